```python
import jax, jax.numpy as jnp
from jax import lax
import numpy as np

D_MODEL = 1024
BATCH = 4
SEQ = 8192
DEPTH = 1
DEC_BATCH = 32
DEC_SEQ = 2048
PAST_LEN = 128

D_MIX = 2 * D_MODEL
FOURIER_W = D_MIX // 4
FOURIER_GROUPS = 8
FOURIER_GROUP_DIM = FOURIER_W // FOURIER_GROUPS
SSD_W = D_MIX - FOURIER_W
SSD_HEAD_DIM = 64
SSD_HEADS = SSD_W // SSD_HEAD_DIM
SSD_GROUPS = 4
SSD_HPG = SSD_HEADS // SSD_GROUPS
D_STATE = 128
CONV_WIDTH = 5
CONV_PAD = CONV_WIDTH // 2
CONV_CH = SSD_W + 2 * SSD_GROUPS * D_STATE
CHUNK = 128
IN_W = FOURIER_W + SSD_W + CONV_CH + 2 * SSD_HEADS
N_MEM = 256
XATTN_HEADS = 4
XATTN_HEAD_DIM = D_MODEL // XATTN_HEADS
N_EXPERTS = 32
TOP_K = 4
D_FF_EXPERT = D_MODEL
SWIGLU_ALPHA = 1.702
SWIGLU_LIMIT = 7.0
MOE_BLOCK = 128
NORM_EPS = 1e-5

kernel_name = "hymba_fnet_ssd_xattn_moe_encoder"


def _rms_norm(x, g):
    xf = x.astype(jnp.float32)
    y = xf * lax.rsqrt(jnp.mean(xf * xf, axis=-1, keepdims=True) + NORM_EPS)
    return (y * g.astype(jnp.float32)).astype(x.dtype)


def _ssd_scan(x, dt, a, b_in, c_in):
    bsz, seqlen = x.shape[:2]
    nc = seqlen // CHUNK

    def chunked(t):
        return t.reshape((bsz, nc, CHUNK) + t.shape[2:]).swapaxes(0, 1)

    mask = jnp.tril(jnp.ones((CHUNK, CHUNK), dtype=bool))[None, :, :, None, None]

    def step(state, inp):
        xc, dtc, bc, cc = inp
        acum = jnp.cumsum(dtc * a, axis=1)
        seg = acum[:, :, None] - acum[:, None, :]
        decay = jnp.exp(jnp.where(mask, seg, -jnp.inf))
        xdt = xc * dtc[..., None]
        cb = jnp.einsum('blgn,bsgn->blsg', cc, bc)
        y_diag = jnp.einsum('blsg,blsgr,bsgrp->blgrp', cb, decay, xdt)
        y_off = jnp.einsum('blgn,bgrpn->blgrp', cc, state) * jnp.exp(acum)[..., None]
        to_end = jnp.exp(acum[:, -1:] - acum)
        new_state = (state * jnp.exp(acum[:, -1])[..., None, None]
                     + jnp.einsum('bsgn,bsgrp->bgrpn', bc, xdt * to_end[..., None]))
        return new_state, y_diag + y_off

    state0 = jnp.zeros((bsz, SSD_GROUPS, SSD_HPG, SSD_HEAD_DIM, D_STATE), jnp.float32)
    _, ys = lax.scan(step, state0, (chunked(x), chunked(dt), chunked(b_in), chunked(c_in)))
    return ys.swapaxes(0, 1).reshape(x.shape)


def _mixer(xn, w_in, conv_w, conv_b, a_log_fwd, a_log_bwd, dt_bias_fwd, dt_bias_bwd,
           d_skip, g_fourier_norm, g_ssd_norm, w_out):
    bsz, seqlen, _ = xn.shape
    proj = xn @ w_in
    s1 = FOURIER_W
    s2 = s1 + SSD_W
    s3 = s2 + CONV_CH
    u, z, xbc, dt_raw = proj[..., :s1], proj[..., s1:s2], proj[..., s2:s3], proj[..., s3:]

    uf = u.astype(jnp.float32).reshape(bsz, seqlen, FOURIER_GROUPS, FOURIER_GROUP_DIM)
    four = jnp.fft.fft2(uf, axes=(1, 3), norm='ortho').real.reshape(bsz, seqlen, FOURIER_W)
    four = _rms_norm(four.astype(xn.dtype), g_fourier_norm)

    xbc = lax.conv_general_dilated(xbc, conv_w, window_strides=(1,), padding=[(CONV_PAD, CONV_PAD)],
                                   dimension_numbers=('NWC', 'WIO', 'NWC'),
                                   feature_group_count=CONV_CH) + conv_b
    xbc = jax.nn.silu(xbc).astype(jnp.float32)
    gn = SSD_GROUPS * D_STATE
    xs = xbc[..., :SSD_W].reshape(bsz, seqlen, SSD_GROUPS, SSD_HPG, SSD_HEAD_DIM)
    bm = xbc[..., SSD_W:SSD_W + gn].reshape(bsz, seqlen, SSD_GROUPS, D_STATE)
    cm = xbc[..., SSD_W + gn:].reshape(bsz, seqlen, SSD_GROUPS, D_STATE)
    dtf = dt_raw.astype(jnp.float32)
    dt_f = jax.nn.softplus(dtf[..., :SSD_HEADS] + dt_bias_fwd.astype(jnp.float32)).reshape(
        bsz, seqlen, SSD_GROUPS, SSD_HPG)
    dt_b = jax.nn.softplus(dtf[..., SSD_HEADS:] + dt_bias_bwd.astype(jnp.float32)).reshape(
        bsz, seqlen, SSD_GROUPS, SSD_HPG)
    a_f = -jnp.exp(a_log_fwd.astype(jnp.float32)).reshape(SSD_GROUPS, SSD_HPG)
    a_b = -jnp.exp(a_log_bwd.astype(jnp.float32)).reshape(SSD_GROUPS, SSD_HPG)
    y_f = _ssd_scan(xs, dt_f, a_f, bm, cm)
    flip = lambda t: jnp.flip(t, axis=1)
    y_b = flip(_ssd_scan(flip(xs), flip(dt_b), a_b, flip(bm), flip(cm)))
    d = d_skip.astype(jnp.float32).reshape(SSD_GROUPS, SSD_HPG)[..., None]
    y = (y_f + y_b + d * xs).reshape(bsz, seqlen, SSD_W)
    y = y * jax.nn.silu(z.astype(jnp.float32))
    y = _rms_norm(y, g_ssd_norm).astype(xn.dtype)

    return jnp.concatenate([four, y], axis=-1) @ w_out


def _cross_attn(hn, memn, w_q, w_k, w_v, w_o):
    bsz, seqlen, _ = hn.shape
    q = (hn @ w_q).reshape(bsz, seqlen, XATTN_HEADS, XATTN_HEAD_DIM)
    k = (memn @ w_k).reshape(bsz, -1, XATTN_HEADS, XATTN_HEAD_DIM)
    v = (memn @ w_v).reshape(bsz, -1, XATTN_HEADS, XATTN_HEAD_DIM)
    scores = jnp.einsum('bshd,bmhd->bhsm', q, k).astype(jnp.float32) * (XATTN_HEAD_DIM ** -0.5)
    p = jax.nn.softmax(scores, axis=-1).astype(v.dtype)
    o = jnp.einsum('bhsm,bmhd->bshd', p, v).reshape(bsz, seqlen, D_MODEL)
    return o @ w_o


def _moe(hn, w_router, b_router, w_mlp1, b_mlp1, w_mlp2, b_mlp2):
    bsz, seqlen, _ = hn.shape
    xf = hn.reshape(-1, D_MODEL)
    n_tok = xf.shape[0]
    logits = (xf @ w_router + b_router).astype(jnp.float32)
    top_v, top_i = lax.top_k(logits, TOP_K)
    gates = jax.nn.softmax(top_v, axis=-1)

    n_assign = n_tok * TOP_K
    n_blocks = -(-n_assign // MOE_BLOCK) + N_EXPERTS
    n_slots = n_blocks * MOE_BLOCK
    flat_e = top_i.reshape(-1).astype(jnp.int32)
    flat_g = gates.reshape(-1).astype(hn.dtype)
    order = jnp.argsort(flat_e)
    sorted_e = flat_e[order]
    counts = jnp.bincount(flat_e, length=N_EXPERTS)
    padded = ((counts + MOE_BLOCK - 1) // MOE_BLOCK) * MOE_BLOCK
    pend = jnp.cumsum(padded)
    pstart = pend - padded
    start = jnp.cumsum(counts) - counts
    dest = pstart[sorted_e] + jnp.arange(n_assign, dtype=jnp.int32) - start[sorted_e]
    slot_tok = jnp.full((n_slots,), n_tok, jnp.int32).at[dest].set((order // TOP_K).astype(jnp.int32))
    slot_gate = jnp.zeros((n_slots,), hn.dtype).at[dest].set(flat_g[order])
    block_e = jnp.minimum(jnp.searchsorted(pend, jnp.arange(n_blocks) * MOE_BLOCK, side='right'),
                          N_EXPERTS - 1).astype(jnp.int32)
    x_pad = jnp.concatenate([xf, jnp.zeros((1, D_MODEL), xf.dtype)], axis=0)

    def expert_block(args):
        tok, e = args
        xb = x_pad[tok]
        h = xb @ w_mlp1[e] + b_mlp1[e]
        glu = jnp.minimum(h[:, :D_FF_EXPERT], SWIGLU_LIMIT)
        lin = jnp.clip(h[:, D_FF_EXPERT:], -SWIGLU_LIMIT, SWIGLU_LIMIT)
        act = glu * jax.nn.sigmoid(SWIGLU_ALPHA * glu) * (lin + 1.0)
        return act @ w_mlp2[e] + b_mlp2[e]

    outs = lax.map(expert_block, (slot_tok.reshape(n_blocks, MOE_BLOCK), block_e))
    outs = outs.reshape(n_slots, D_MODEL) * slot_gate[:, None]
    y = jnp.zeros((n_tok + 1, D_MODEL), outs.dtype).at[slot_tok].add(outs)[:n_tok]
    return y.reshape(bsz, seqlen, D_MODEL)


def _encoder(x, mem, layer_params, g_final):
    h = x
    for i in range(DEPTH):
        (g_mix_norm, w_in, conv_w, conv_b, a_log_fwd, a_log_bwd, dt_bias_fwd, dt_bias_bwd, d_skip,
         g_fourier_norm, g_ssd_norm, w_out, g_xattn_norm, g_mem_norm, w_q, w_k, w_v, w_o,
         g_moe_norm, w_router, b_router, w_mlp1, b_mlp1, w_mlp2, b_mlp2) = [t[i] for t in layer_params]
        h = h + _mixer(_rms_norm(h, g_mix_norm), w_in, conv_w, conv_b, a_log_fwd, a_log_bwd,
                       dt_bias_fwd, dt_bias_bwd, d_skip, g_fourier_norm, g_ssd_norm, w_out)
        h = h + _cross_attn(_rms_norm(h, g_xattn_norm), _rms_norm(mem, g_mem_norm), w_q, w_k, w_v, w_o)
        h = h + _moe(_rms_norm(h, g_moe_norm), w_router, b_router, w_mlp1, b_mlp1, w_mlp2, b_mlp2)
    return _rms_norm(h, g_final)


def setup_inputs(seed: int = 0) -> dict:
    key = jax.random.key(seed)
    ks = jax.random.split(key, 32)
    f32 = jnp.float32
    nrm = lambda k, shape, scale: jax.random.normal(k, shape, f32) * scale
    gain = lambda k, shape: 1.0 + 0.02 * jax.random.normal(k, shape, f32)
    L = DEPTH
    dt_init = lambda k: jnp.exp(jax.random.uniform(k, (L, SSD_HEADS), f32, np.log(1e-3), np.log(1e-1)))
    dt_f = dt_init(ks[8])
    dt_b = dt_init(ks[9])
    return {
        "x_prompt": nrm(ks[0], (BATCH, SEQ, D_MODEL), 1.0),
        "x_sample": nrm(ks[1], (DEC_BATCH, DEC_SEQ, D_MODEL), 1.0),
        "mem_prompt": nrm(ks[2], (BATCH, N_MEM, D_MODEL), 1.0),
        "mem_sample": nrm(ks[3], (DEC_BATCH, N_MEM, D_MODEL), 1.0),
        "g_mix_norm": gain(ks[4], (L, D_MODEL)),
        "w_in": nrm(ks[5], (L, D_MODEL, IN_W), D_MODEL ** -0.5),
        "conv_w": nrm(ks[6], (L, CONV_WIDTH, 1, CONV_CH), CONV_WIDTH ** -0.5),
        "conv_b": nrm(ks[7], (L, CONV_CH), 0.01),
        "a_log_fwd": jnp.log(jax.random.uniform(ks[10], (L, SSD_HEADS), f32, 1.0, 16.0)),
        "a_log_bwd": jnp.log(jax.random.uniform(ks[11], (L, SSD_HEADS), f32, 1.0, 16.0)),
        "dt_bias_fwd": dt_f + jnp.log(-jnp.expm1(-dt_f)),
        "dt_bias_bwd": dt_b + jnp.log(-jnp.expm1(-dt_b)),
        "d_skip": 1.0 + 0.1 * jax.random.normal(ks[12], (L, SSD_HEADS), f32),
        "g_fourier_norm": gain(ks[13], (L, FOURIER_W)),
        "g_ssd_norm": gain(ks[14], (L, SSD_W)),
        "w_out": nrm(ks[15], (L, D_MIX, D_MODEL), D_MIX ** -0.5),
        "g_xattn_norm": gain(ks[16], (L, D_MODEL)),
        "g_mem_norm": gain(ks[17], (L, D_MODEL)),
        "w_q": nrm(ks[18], (L, D_MODEL, D_MODEL), D_MODEL ** -0.5),
        "w_k": nrm(ks[19], (L, D_MODEL, D_MODEL), D_MODEL ** -0.5),
        "w_v": nrm(ks[20], (L, D_MODEL, D_MODEL), D_MODEL ** -0.5),
        "w_o": nrm(ks[21], (L, D_MODEL, D_MODEL), D_MODEL ** -0.5),
        "g_moe_norm": gain(ks[22], (L, D_MODEL)),
        "w_router": nrm(ks[23], (L, D_MODEL, N_EXPERTS), D_MODEL ** -0.5),
        "b_router": nrm(ks[24], (L, N_EXPERTS), 0.01),
        "w_mlp1": nrm(ks[25], (L, N_EXPERTS, D_MODEL, 2 * D_FF_EXPERT), D_MODEL ** -0.5),
        "b_mlp1": nrm(ks[26], (L, N_EXPERTS, 2 * D_FF_EXPERT), 0.01),
        "w_mlp2": nrm(ks[27], (L, N_EXPERTS, D_FF_EXPERT, D_MODEL), D_FF_EXPERT ** -0.5),
        "b_mlp2": nrm(ks[28], (L, N_EXPERTS, D_MODEL), 0.01),
        "g_final": gain(ks[29], (D_MODEL,)),
    }


def reference(x_prompt, x_sample, mem_prompt, mem_sample, g_mix_norm, w_in, conv_w, conv_b,
              a_log_fwd, a_log_bwd, dt_bias_fwd, dt_bias_bwd, d_skip, g_fourier_norm, g_ssd_norm,
              w_out, g_xattn_norm, g_mem_norm, w_q, w_k, w_v, w_o, g_moe_norm, w_router, b_router,
              w_mlp1, b_mlp1, w_mlp2, b_mlp2, g_final):
    layer_params = (g_mix_norm, w_in, conv_w, conv_b, a_log_fwd, a_log_bwd, dt_bias_fwd, dt_bias_bwd,
                    d_skip, g_fourier_norm, g_ssd_norm, w_out, g_xattn_norm, g_mem_norm, w_q, w_k,
                    w_v, w_o, g_moe_norm, w_router, b_router, w_mlp1, b_mlp1, w_mlp2, b_mlp2)
    y_prompt = _encoder(x_prompt, mem_prompt, layer_params, g_final)
    y_sample = _encoder(x_sample, mem_sample, layer_params, g_final)
    return (y_prompt, y_sample)
```

```python
import functools

import numpy as np
import jax
import jax.numpy as jnp
from jax import lax
from jax.experimental import pallas as pl
from jax.experimental.pallas import tpu as pltpu

F32 = jnp.float32
BF16 = jnp.bfloat16

D_MODEL = 1024
D_MIX = 2 * D_MODEL
FOURIER_W = D_MIX // 4
FOURIER_GROUPS = 8
FOURIER_GROUP_DIM = FOURIER_W // FOURIER_GROUPS
SSD_W = D_MIX - FOURIER_W
SSD_HEAD_DIM = 64
SSD_HEADS = SSD_W // SSD_HEAD_DIM
SSD_GROUPS = 4
SSD_HPG = SSD_HEADS // SSD_GROUPS
D_STATE = 128
CONV_WIDTH = 5
CONV_PAD = CONV_WIDTH // 2
GN = SSD_GROUPS * D_STATE
CONV_CH = SSD_W + 2 * GN
CHUNK = 128
XATTN_HEADS = 4
XATTN_HEAD_DIM = D_MODEL // XATTN_HEADS
N_EXPERTS = 32
TOP_K = 4
D_FF = D_MODEL
SWIGLU_ALPHA = 1.702
SWIGLU_LIMIT = 7.0
NORM_EPS = 1e-5

LANES = 128
HALO = 16
MOE_BLK = 512
VMEM_LIMIT = 48 * 1024 * 1024


def _cparams(sem):
    return pltpu.CompilerParams(dimension_semantics=sem, vmem_limit_bytes=VMEM_LIMIT)


def _rms(x, g):
    return x * lax.rsqrt(jnp.mean(x * x, axis=-1, keepdims=True) + NORM_EPS) * g


def _sigmoid(x):
    return 1.0 / (1.0 + jnp.exp(-x))


def _dot(a, b):
    return jnp.dot(a, b, preferred_element_type=F32)


def _dot_nt(a, b):
    return lax.dot_general(a, b, (((1,), (1,)), ((), ())), preferred_element_type=F32)


def _split3(x):
    h = x.astype(BF16)
    r = x - h.astype(F32)
    m = r.astype(BF16)
    l = (r - m.astype(F32)).astype(BF16)
    return h, m, l


def _inproj_kernel(x_ref, g_ref, wu_ref, wz_ref, wx_ref, wdt_ref, cs_ref,
                   ab_ref, z_ref, xbc_ref, dt_ref):
    xn = _rms(x_ref[...], g_ref[...]).astype(BF16)
    u = _dot(xn, wu_ref[...])
    ab_ref[...] = _dot(u.astype(BF16), cs_ref[...]).astype(BF16)
    z_ref[...] = _dot(xn, wz_ref[...]).astype(BF16)
    xbc_ref[...] = _dot(xn, wx_ref[...]).astype(BF16)
    dt_ref[...] = _dot(xn, wdt_ref[...])


def _inproj(x2d, g, wu, wz, wx, wdt, cs, tm):
    t = x2d.shape[0]
    full = lambda a: pl.BlockSpec(a.shape, lambda i: (0,) * a.ndim)
    row = lambda w: pl.BlockSpec((tm, w), lambda i: (i, 0))
    return pl.pallas_call(
        _inproj_kernel,
        grid=(t // tm,),
        in_specs=[row(D_MODEL), full(g), full(wu), full(wz), full(wx), full(wdt), full(cs)],
        out_specs=[row(2 * FOURIER_W), row(SSD_W), row(CONV_CH), row(2 * LANES)],
        out_shape=[jax.ShapeDtypeStruct((t, 2 * FOURIER_W), BF16),
                   jax.ShapeDtypeStruct((t, SSD_W), BF16),
                   jax.ShapeDtypeStruct((t, CONV_CH), BF16),
                   jax.ShapeDtypeStruct((t, 2 * LANES), F32)],
        compiler_params=_cparams(("parallel",)),
        name="inproj",
    )(x2d, g, wu, wz, wx, wdt, cs)


def _fourier_kernel(c_ref, s_ref, ab_ref, g_ref, o_ref, acc_ref, *, scale, nb):
    k = pl.program_id(2)

    @pl.when(k == 0)
    def _():
        acc_ref[...] = jnp.zeros_like(acc_ref)

    c = c_ref[...]
    s = s_ref[...]
    for b in range(nb):
        acc_ref[b] += _dot(c, ab_ref[b, :, :FOURIER_W]) + _dot(s, ab_ref[b, :, FOURIER_W:])

    @pl.when(k == pl.num_programs(2) - 1)
    def _():
        for b in range(nb):
            o_ref[b] = _rms(acc_ref[b] * scale, g_ref[...]).astype(BF16)


def _fourier(ab, cos_t, nsin_t, g, nb, tr, tk):
    bsz, seqlen, _ = ab.shape
    scale = float(1.0 / np.sqrt(seqlen * FOURIER_GROUP_DIM))
    return pl.pallas_call(
        functools.partial(_fourier_kernel, scale=scale, nb=nb),
        grid=(bsz // nb, seqlen // tr, seqlen // tk),
        in_specs=[pl.BlockSpec((tr, tk), lambda b, i, k: (i, k)),
                  pl.BlockSpec((tr, tk), lambda b, i, k: (i, k)),
                  pl.BlockSpec((nb, tk, 2 * FOURIER_W), lambda b, i, k: (b, k, 0)),
                  pl.BlockSpec((1, FOURIER_W), lambda b, i, k: (0, 0))],
        out_specs=pl.BlockSpec((nb, tr, FOURIER_W), lambda b, i, k: (b, i, 0)),
        out_shape=jax.ShapeDtypeStruct((bsz, seqlen, FOURIER_W), BF16),
        scratch_shapes=[pltpu.VMEM((nb, tr, FOURIER_W), F32)],
        compiler_params=_cparams(("parallel", "parallel", "arbitrary")),
        name="fourier",
    )(cos_t, nsin_t, ab, g)


def _dft_tables(seqlen):
    lo = min(seqlen, LANES)
    hi = seqlen // lo
    k = np.arange(seqlen, dtype=np.int64)
    a_hi = 2.0 * np.pi * ((np.arange(hi, dtype=np.int64)[:, None] * lo * k[None, :]) % seqlen) / seqlen
    a_lo = 2.0 * np.pi * ((np.arange(lo, dtype=np.int64)[:, None] * k[None, :]) % seqlen) / seqlen
    c1 = jnp.asarray(np.cos(a_hi), F32)[:, None, :]
    s1 = jnp.asarray(np.sin(a_hi), F32)[:, None, :]
    c0 = jnp.asarray(np.cos(a_lo), F32)[None, :, :]
    s0 = jnp.asarray(np.sin(a_lo), F32)[None, :, :]
    cos_t = (c1 * c0 - s1 * s0).reshape(seqlen, seqlen).astype(BF16)
    nsin_t = (-(s1 * c0 + c1 * s0)).reshape(seqlen, seqlen).astype(BF16)
    return cos_t, nsin_t


def _channel_dft_table():
    c = np.arange(FOURIER_W)
    same = (c[:, None] // FOURIER_GROUP_DIM) == (c[None, :] // FOURIER_GROUP_DIM)
    ang = 2.0 * np.pi * ((c[:, None] % FOURIER_GROUP_DIM) * (c[None, :] % FOURIER_GROUP_DIM)
                         % FOURIER_GROUP_DIM) / FOURIER_GROUP_DIM
    cc = np.where(same, np.cos(ang), 0.0)
    sc = np.where(same, np.sin(ang), 0.0)
    return jnp.asarray(np.concatenate([cc, sc], axis=1), BF16)


def _conv_kernel(prev_ref, main_ref, next_ref, w_ref, b_ref, o_ref):
    i = pl.program_id(1)
    tc = main_ref.shape[1]
    prev = jnp.where(i > 0, prev_ref[0].astype(F32), 0.0)
    nxt = jnp.where(i < pl.num_programs(1) - 1, next_ref[0].astype(F32), 0.0)
    xe = jnp.concatenate([prev, main_ref[0].astype(F32), nxt], axis=0)
    rows = tc + 2 * HALO
    acc = b_ref[...] + w_ref[CONV_PAD:CONV_PAD + 1, :] * xe
    for w in range(CONV_WIDTH):
        if w == CONV_PAD:
            continue
        acc = acc + w_ref[w:w + 1, :] * pltpu.roll(xe, (CONV_PAD - w) % rows, axis=0)
    y = acc[HALO:HALO + tc]
    o_ref[0] = (y * _sigmoid(y)).astype(BF16)


def _conv(xbc, w, b, tc, cw):
    bsz, seqlen, _ = xbc.shape
    r = tc // HALO
    nh = seqlen // HALO
    return pl.pallas_call(
        _conv_kernel,
        grid=(bsz, seqlen // tc, CONV_CH // cw),
        in_specs=[pl.BlockSpec((1, HALO, cw), lambda bi, i, c: (bi, jnp.maximum(i * r - 1, 0), c)),
                  pl.BlockSpec((1, tc, cw), lambda bi, i, c: (bi, i, c)),
                  pl.BlockSpec((1, HALO, cw), lambda bi, i, c: (bi, jnp.minimum((i + 1) * r, nh - 1), c)),
                  pl.BlockSpec((CONV_WIDTH, cw), lambda bi, i, c: (0, c)),
                  pl.BlockSpec((1, cw), lambda bi, i, c: (0, c))],
        out_specs=pl.BlockSpec((1, tc, cw), lambda bi, i, c: (bi, i, c)),
        out_shape=jax.ShapeDtypeStruct(xbc.shape, BF16),
        compiler_params=_cparams(("parallel", "parallel", "parallel")),
        name="conv",
    )(xbc, xbc, xbc, w, b)


def _ssd_kernel(xbc_ref, dt_ref, tri_ref, alog_ref, dtb_ref, dsk_ref, y_ref, state_ref):
    q = CHUNK

    @pl.when(pl.program_id(2) == 0)
    def _():
        state_ref[...] = jnp.zeros_like(state_ref)

    tri = tri_ref[0]
    mask = tri > 0.5
    a = -jnp.exp(alog_ref[0])
    dtr = dt_ref[0] + dtb_ref[0]
    dt = jnp.maximum(dtr, 0.0) + jnp.log1p(jnp.exp(-jnp.abs(dtr)))
    da = dt * a
    tri_b = tri.astype(BF16)
    d0, d1, d2 = _split3(da)
    acum = _dot(tri_b, d0) + _dot(tri_b, d1) + _dot(tri_b, d2)
    total = jnp.sum(da, axis=0, keepdims=True)
    eacum = jnp.exp(acum)
    etot = jnp.exp(total)
    acum_t = acum.T
    dt_t = dt.T
    w_t = (dt * jnp.exp(total - acum)).T
    lane_lo = lax.broadcasted_iota(jnp.int32, (1, LANES), 1) < SSD_HEAD_DIM
    zero_b = jnp.zeros((), BF16)

    for g in range(SSD_GROUPS):
        bm = xbc_ref[0, :, SSD_W + g * D_STATE:SSD_W + (g + 1) * D_STATE]
        cm = xbc_ref[0, :, SSD_W + GN + g * D_STATE:SSD_W + GN + (g + 1) * D_STATE]
        cb = _dot_nt(cm, bm)
        cm_f = cm.astype(F32)
        bm_t = bm.astype(F32).T
        for p in range(SSD_HPG // 2):
            t = g * (SSD_HPG // 2) + p
            cols = slice(t * LANES, (t + 1) * LANES)
            xs = xbc_ref[0, :, cols]
            st = state_ref[:, cols]
            xs_lo = jnp.where(lane_lo, xs, zero_b)
            xs_hi = jnp.where(lane_lo, zero_b, xs)
            st_b = st.astype(BF16)
            st_lo = jnp.where(lane_lo, st_b, zero_b)
            st_hi = jnp.where(lane_lo, zero_b, st_b)
            lhs = []
            lhs2 = []
            for h in (2 * t, 2 * t + 1):
                colb = jnp.broadcast_to(acum[:, h:h + 1], (q, q))
                decay = jnp.where(mask, jnp.exp(colb - acum_t[h:h + 1, :]), 0.0)
                lhs.append((cb * decay * dt_t[h:h + 1, :]).astype(BF16))
                lhs.append((cm_f * jnp.broadcast_to(eacum[:, h:h + 1], (q, D_STATE))).astype(BF16))
                lhs2.append((bm_t * w_t[h:h + 1, :]).astype(BF16))
            y = _dot(jnp.concatenate(lhs, axis=1),
                     jnp.concatenate([xs_lo, st_lo, xs_hi, st_hi], axis=0))
            y = y + dsk_ref[0, :, cols] * xs.astype(F32)
            y_ref[0, 0, :, cols] = y.astype(BF16)
            dec = jnp.where(lane_lo, etot[:, 2 * t:2 * t + 1], etot[:, 2 * t + 1:2 * t + 2])
            state_ref[:, cols] = st * dec + _dot(jnp.concatenate(lhs2, axis=1),
                                                 jnp.concatenate([xs_lo, xs_hi], axis=0))


def _ssd(xbc, dt, tri, alog, dtb, dsk):
    bsz, seqlen, _ = xbc.shape
    nc = seqlen // CHUNK
    chunk = lambda d, j: j + d * (nc - 1 - 2 * j)
    return pl.pallas_call(
        _ssd_kernel,
        grid=(bsz, 2, nc),
        in_specs=[pl.BlockSpec((1, CHUNK, CONV_CH), lambda b, d, j: (b, chunk(d, j), 0)),
                  pl.BlockSpec((1, CHUNK, LANES), lambda b, d, j: (b, chunk(d, j), d)),
                  pl.BlockSpec((1, CHUNK, CHUNK), lambda b, d, j: (d, 0, 0)),
                  pl.BlockSpec((1, 1, LANES), lambda b, d, j: (d, 0, 0)),
                  pl.BlockSpec((1, 1, LANES), lambda b, d, j: (d, 0, 0)),
                  pl.BlockSpec((1, 1, SSD_W), lambda b, d, j: (d, 0, 0))],
        out_specs=pl.BlockSpec((1, 1, CHUNK, SSD_W), lambda b, d, j: (d, b, chunk(d, j), 0)),
        out_shape=jax.ShapeDtypeStruct((2, bsz, seqlen, SSD_W), BF16),
        scratch_shapes=[pltpu.VMEM((D_STATE, SSD_W), F32)],
        compiler_params=_cparams(("parallel", "arbitrary", "arbitrary")),
        name="ssd",
    )(xbc, dt, tri, alog, dtb, dsk)


def _outproj_kernel(x_ref, four_ref, y_ref, z_ref, gs_ref, w4_ref, wy_ref, o_ref):
    y = y_ref[0].astype(F32) + y_ref[1].astype(F32)
    z = z_ref[...].astype(F32)
    y = y * (z * _sigmoid(z))
    yn = _rms(y, gs_ref[...]).astype(BF16)
    o_ref[...] = x_ref[...] + _dot(four_ref[...], w4_ref[...]) + _dot(yn, wy_ref[...])


def _outproj(x2d, four2d, y2, z2d, gs, w4, wy, tm):
    t = x2d.shape[0]
    full = lambda a: pl.BlockSpec(a.shape, lambda i: (0,) * a.ndim)
    row = lambda w: pl.BlockSpec((tm, w), lambda i: (i, 0))
    return pl.pallas_call(
        _outproj_kernel,
        grid=(t // tm,),
        in_specs=[row(D_MODEL), row(FOURIER_W), pl.BlockSpec((2, tm, SSD_W), lambda i: (0, i, 0)),
                  row(SSD_W), full(gs), full(w4), full(wy)],
        out_specs=row(D_MODEL),
        out_shape=jax.ShapeDtypeStruct((t, D_MODEL), F32),
        compiler_params=_cparams(("parallel",)),
        name="outproj",
    )(x2d, four2d, y2, z2d, gs, w4, wy)


def _kv_kernel(mem_ref, g_ref, wk_ref, wv_ref, k_ref, v_ref):
    mn = _rms(mem_ref[0], g_ref[...]).astype(BF16)
    k_ref[0] = _dot(mn, wk_ref[...]).astype(BF16)
    v_ref[0] = _dot(mn, wv_ref[...]).astype(BF16)


def _kv(mem, g, wk, wv):
    bsz, nm, _ = mem.shape
    full = lambda a: pl.BlockSpec(a.shape, lambda b: (0,) * a.ndim)
    blk = pl.BlockSpec((1, nm, D_MODEL), lambda b: (b, 0, 0))
    return pl.pallas_call(
        _kv_kernel,
        grid=(bsz,),
        in_specs=[blk, full(g), full(wk), full(wv)],
        out_specs=[blk, blk],
        out_shape=[jax.ShapeDtypeStruct(mem.shape, BF16)] * 2,
        compiler_params=_cparams(("parallel",)),
        name="kv",
    )(mem, g, wk, wv)


def _xattn_kernel(h_ref, g_ref, wq_ref, k_ref, v_ref, wo_ref, o_ref):
    h = h_ref[0]
    hn = _rms(h, g_ref[...]).astype(BF16)
    q = (_dot(hn, wq_ref[...]) * (XATTN_HEAD_DIM ** -0.5)).astype(BF16)
    outs = []
    for hd in range(XATTN_HEADS):
        cols = slice(hd * XATTN_HEAD_DIM, (hd + 1) * XATTN_HEAD_DIM)
        s = _dot_nt(q[:, cols], k_ref[0, :, cols])
        e = jnp.exp(s - jnp.max(s, axis=-1, keepdims=True))
        p = (e / jnp.sum(e, axis=-1, keepdims=True)).astype(BF16)
        outs.append(_dot(p, v_ref[0, :, cols]).astype(BF16))
    o_ref[0] = h + _dot(jnp.concatenate(outs, axis=1), wo_ref[...])


def _xattn(h, g, wq, k, v, wo, tm):
    bsz, seqlen, _ = h.shape
    nm = k.shape[1]
    full = lambda a: pl.BlockSpec(a.shape, lambda b, i: (0,) * a.ndim)
    tok = pl.BlockSpec((1, tm, D_MODEL), lambda b, i: (b, i, 0))
    mem = pl.BlockSpec((1, nm, D_MODEL), lambda b, i: (b, 0, 0))
    return pl.pallas_call(
        _xattn_kernel,
        grid=(bsz, seqlen // tm),
        in_specs=[tok, full(g), full(wq), mem, mem, full(wo)],
        out_specs=tok,
        out_shape=jax.ShapeDtypeStruct(h.shape, F32),
        compiler_params=_cparams(("parallel", "parallel")),
        name="xattn",
    )(h, g, wq, k, v, wo)


def _router_kernel(h_ref, g_ref, wr_ref, br_ref, xn_ref, idx_ref, gate_ref, rank_ref, cnt_ref,
                   carry_ref):
    tm = h_ref.shape[0]

    @pl.when(pl.program_id(0) == 0)
    def _():
        carry_ref[...] = jnp.zeros_like(carry_ref)

    xn = _rms(h_ref[...], g_ref[...])
    xn_ref[...] = xn
    x0, x1, _ = _split3(xn)
    w0, w1, _ = _split3(wr_ref[...])
    logits = _dot_nt(w0, x0) + _dot_nt(w0, x1) + _dot_nt(w1, x0) + br_ref[...]

    eid = lax.broadcasted_iota(jnp.int32, (N_EXPERTS, tm), 0)
    work = logits
    vals, hits = [], []
    sel = jnp.zeros((N_EXPERTS, tm), F32)
    for k in range(TOP_K):
        m = jnp.max(work, axis=0, keepdims=True)
        ik = jnp.min(jnp.where(work == m, eid, N_EXPERTS), axis=0, keepdims=True)
        hit = eid == ik
        idx_ref[k:k + 1, :] = ik
        vals.append(m)
        hits.append(hit)
        sel = jnp.where(hit, 1.0, sel)
        work = jnp.where(hit, -jnp.inf, work)
    es = [jnp.exp(v - vals[0]) for v in vals]
    den = es[0] + es[1] + es[2] + es[3]
    for k in range(TOP_K):
        gate_ref[k:k + 1, :] = es[k] / den

    r = lax.broadcasted_iota(jnp.int32, (tm, tm), 0)
    c = lax.broadcasted_iota(jnp.int32, (tm, tm), 1)
    before = jnp.where(r < c, 1.0, 0.0).astype(BF16)
    cum = _dot(sel.astype(BF16), before) + carry_ref[:, 0:1]
    for k in range(TOP_K):
        rk = jnp.sum(jnp.where(hits[k], cum, 0.0), axis=0, keepdims=True)
        rank_ref[k:k + 1, :] = rk.astype(jnp.int32)
    carry_ref[...] = carry_ref[...] + jnp.sum(sel, axis=1, keepdims=True)
    cnt_ref[...] = carry_ref[...].astype(jnp.int32)


def _router(h2d, g, wr_t, br, tm):
    t = h2d.shape[0]
    full = lambda a: pl.BlockSpec(a.shape, lambda i: (0,) * a.ndim)
    kt = pl.BlockSpec((TOP_K, tm), lambda i: (0, i))
    return pl.pallas_call(
        _router_kernel,
        grid=(t // tm,),
        in_specs=[pl.BlockSpec((tm, D_MODEL), lambda i: (i, 0)), full(g), full(wr_t), full(br)],
        out_specs=[pl.BlockSpec((tm, D_MODEL), lambda i: (i, 0)), kt, kt, kt,
                   pl.BlockSpec((N_EXPERTS, LANES), lambda i: (0, 0))],
        out_shape=[jax.ShapeDtypeStruct((t, D_MODEL), F32),
                   jax.ShapeDtypeStruct((TOP_K, t), jnp.int32),
                   jax.ShapeDtypeStruct((TOP_K, t), F32),
                   jax.ShapeDtypeStruct((TOP_K, t), jnp.int32),
                   jax.ShapeDtypeStruct((N_EXPERTS, LANES), jnp.int32)],
        scratch_shapes=[pltpu.VMEM((N_EXPERTS, LANES), F32)],
        compiler_params=_cparams(("arbitrary",)),
        name="router",
    )(h2d, g, wr_t, br)


def _index_copies(i, n, idx_hbm, rank_hbm, idx_s, rank_s, sem_i):
    src = pl.ds(pl.multiple_of(i * n, n), n)
    return (pltpu.make_async_copy(idx_hbm.at[src], idx_s, sem_i.at[0]),
            pltpu.make_async_copy(rank_hbm.at[src], rank_s, sem_i.at[1]))


def _dispatch_kernel(pstart_ref, zstart_ref, nzero_ref, idx_hbm, rank_hbm, x_ref, xs_hbm,
                     idx_s, rank_s, zbuf, sem_i, sem_z, sem_r):
    i = pl.program_id(0)
    td = x_ref.shape[0]
    n = TOP_K * td
    copies = _index_copies(i, n, idx_hbm, rank_hbm, idx_s, rank_s, sem_i)
    for cp in copies:
        cp.start()

    @pl.when(i == 0)
    def _():
        zbuf[...] = jnp.zeros_like(zbuf)

        def zcopy(j):
            dst = pl.ds(pl.multiple_of(zstart_ref[j], MOE_BLK), MOE_BLK)
            return pltpu.make_async_copy(zbuf, xs_hbm.at[dst], sem_z)

        def issue(j, carry):
            zcopy(j).start()
            return carry

        def drain(j, carry):
            zcopy(j).wait()
            return carry

        lax.fori_loop(0, nzero_ref[0], issue, 0)
        lax.fori_loop(0, nzero_ref[0], drain, 0)

    for cp in copies:
        cp.wait()

    def issue_rows(j, carry):
        t = j & (td - 1)
        d = pstart_ref[idx_s[j]] + rank_s[j]
        pltpu.make_async_copy(x_ref.at[pl.ds(t, 1)], xs_hbm.at[pl.ds(d, 1)], sem_r).start()
        return carry

    lax.fori_loop(0, n, issue_rows, 0, unroll=8)
    for _ in range(TOP_K):
        pltpu.make_async_copy(x_ref, xs_hbm.at[pl.ds(0, td)], sem_r).wait()


def _dispatch(pstart, zstart, nzero, idx_t, rank_t, xn, n_slots, td):
    t = xn.shape[0]
    any_spec = pl.BlockSpec(memory_space=pl.ANY)
    return pl.pallas_call(
        _dispatch_kernel,
        grid_spec=pltpu.PrefetchScalarGridSpec(
            num_scalar_prefetch=3,
            grid=(t // td,),
            in_specs=[any_spec, any_spec,
                      pl.BlockSpec((td, D_MODEL), lambda i, *_: (i, 0))],
            out_specs=any_spec,
            scratch_shapes=[pltpu.SMEM((TOP_K * td,), jnp.int32),
                            pltpu.SMEM((TOP_K * td,), jnp.int32),
                            pltpu.VMEM((MOE_BLK, D_MODEL), F32),
                            pltpu.SemaphoreType.DMA((2,)),
                            pltpu.SemaphoreType.DMA,
                            pltpu.SemaphoreType.DMA]),
        out_shape=jax.ShapeDtypeStruct((n_slots, D_MODEL), F32),
        compiler_params=_cparams(("arbitrary",)),
        name="dispatch",
    )(pstart, zstart, nzero, idx_t, rank_t, xn)


def _expert_kernel(be_ref, na_ref, xs_ref, w1_ref, b1_ref, w2_ref, b2_ref, o_ref):
    i = pl.program_id(0)

    @pl.when(i < na_ref[0])
    def _():
        xb = xs_ref[...].astype(BF16)
        h = _dot(xb, w1_ref[0]) + b1_ref[0]
        glu = jnp.minimum(h[:, :D_FF], SWIGLU_LIMIT)
        lin = jnp.clip(h[:, D_FF:], -SWIGLU_LIMIT, SWIGLU_LIMIT)
        act = glu * _sigmoid(SWIGLU_ALPHA * glu) * (lin + 1.0)
        o_ref[...] = _dot(act.astype(BF16), w2_ref[0]) + b2_ref[0]

    @pl.when(i >= na_ref[0])
    def _():
        o_ref[...] = jnp.zeros_like(o_ref)


def _experts(block_e, n_active, xs, w1, b1, w2, b2):
    n_slots = xs.shape[0]
    return pl.pallas_call(
        _expert_kernel,
        grid_spec=pltpu.PrefetchScalarGridSpec(
            num_scalar_prefetch=2,
            grid=(n_slots // MOE_BLK,),
            in_specs=[pl.BlockSpec((MOE_BLK, D_MODEL), lambda i, be, na: (i, 0)),
                      pl.BlockSpec((1, D_MODEL, 2 * D_FF), lambda i, be, na: (be[i], 0, 0)),
                      pl.BlockSpec((1, 1, 2 * D_FF), lambda i, be, na: (be[i], 0, 0)),
                      pl.BlockSpec((1, D_FF, D_MODEL), lambda i, be, na: (be[i], 0, 0)),
                      pl.BlockSpec((1, 1, D_MODEL), lambda i, be, na: (be[i], 0, 0))],
            out_specs=pl.BlockSpec((MOE_BLK, D_MODEL), lambda i, be, na: (i, 0))),
        out_shape=jax.ShapeDtypeStruct((n_slots, D_MODEL), F32),
        compiler_params=_cparams(("arbitrary",)),
        name="experts",
    )(block_e, n_active, xs, w1, b1, w2, b2)


def _combine_kernel(pstart_ref, idx_hbm, rank_hbm, h_ref, gate_ref, gf_ref, outs_hbm, o_ref,
                    idx_s, rank_s, buf, sem_i, sem_r):
    i = pl.program_id(0)
    tc = h_ref.shape[0]
    n = TOP_K * tc
    copies = _index_copies(i, n, idx_hbm, rank_hbm, idx_s, rank_s, sem_i)
    for cp in copies:
        cp.start()
    for cp in copies:
        cp.wait()

    def issue_rows(j, carry):
        d = pstart_ref[idx_s[j]] + rank_s[j]
        pltpu.make_async_copy(outs_hbm.at[pl.ds(d, 1)], buf.at[pl.ds(j, 1)], sem_r).start()
        return carry

    lax.fori_loop(0, n, issue_rows, 0, unroll=8)

    g0, g1, _ = _split3(gate_ref[...])
    gpack = jnp.concatenate([g0, g1], axis=0)
    r = lax.broadcasted_iota(jnp.int32, (tc, tc), 0)
    c = lax.broadcasted_iota(jnp.int32, (tc, tc), 1)
    eye = jnp.where(r == c, 1.0, 0.0).astype(BF16)
    gcol = _dot_nt(eye, gpack)

    for k in range(TOP_K):
        plane = buf.at[pl.ds(k * tc, tc)]
        pltpu.make_async_copy(outs_hbm.at[pl.ds(0, tc)], plane, sem_r).wait()

    acc = h_ref[...]
    for k in range(TOP_K):
        gk = gcol[:, k:k + 1] + gcol[:, TOP_K + k:TOP_K + k + 1]
        acc = acc + gk * buf[k * tc:(k + 1) * tc, :]
    o_ref[...] = _rms(acc, gf_ref[...])


def _combine(pstart, idx_t, rank_t, h2d, gate, gf, outs, tc):
    t = h2d.shape[0]
    any_spec = pl.BlockSpec(memory_space=pl.ANY)
    return pl.pallas_call(
        _combine_kernel,
        grid_spec=pltpu.PrefetchScalarGridSpec(
            num_scalar_prefetch=1,
            grid=(t // tc,),
            in_specs=[any_spec, any_spec,
                      pl.BlockSpec((tc, D_MODEL), lambda i, *_: (i, 0)),
                      pl.BlockSpec((TOP_K, tc), lambda i, *_: (0, i)),
                      pl.BlockSpec((1, D_MODEL), lambda i, *_: (0, 0)),
                      any_spec],
            out_specs=pl.BlockSpec((tc, D_MODEL), lambda i, *_: (i, 0)),
            scratch_shapes=[pltpu.SMEM((TOP_K * tc,), jnp.int32),
                            pltpu.SMEM((TOP_K * tc,), jnp.int32),
                            pltpu.VMEM((TOP_K * tc, D_MODEL), F32),
                            pltpu.SemaphoreType.DMA((2,)),
                            pltpu.SemaphoreType.DMA]),
        out_shape=jax.ShapeDtypeStruct((t, D_MODEL), F32),
        compiler_params=_cparams(("arbitrary",)),
        name="combine",
    )(pstart, idx_t, rank_t, h2d, gate, gf, outs)


def _tile_rows(idx, tile):
    k, t = idx.shape
    return idx.reshape(k, t // tile, tile).transpose(1, 0, 2).reshape(t * k)


def _prep_weights(g_mix_norm, w_in, conv_w, conv_b, a_log_fwd, a_log_bwd, dt_bias_fwd, dt_bias_bwd,
                  d_skip, g_fourier_norm, g_ssd_norm, w_out, g_xattn_norm, g_mem_norm, w_q, w_k,
                  w_v, w_o, g_moe_norm, w_router, b_router, w_mlp1, b_mlp1, w_mlp2, b_mlp2, g_final):
    s1 = FOURIER_W
    s2 = s1 + SSD_W
    s3 = s2 + CONV_CH
    w = w_in[0]
    wdt = jnp.zeros((D_MODEL, 2 * LANES), F32)
    wdt = wdt.at[:, :SSD_HEADS].set(w[:, s3:s3 + SSD_HEADS])
    wdt = wdt.at[:, LANES:LANES + SSD_HEADS].set(w[:, s3 + SSD_HEADS:])
    pad_heads = lambda a, b: jnp.zeros((2, 1, LANES), F32).at[0, 0, :SSD_HEADS].set(a[0]).at[
        1, 0, :SSD_HEADS].set(b[0])
    dsk = jnp.zeros((2, 1, SSD_W), F32).at[0, 0].set(jnp.repeat(d_skip[0], SSD_HEAD_DIM))
    lower = np.tril(np.ones((CHUNK, CHUNK), np.float32))
    return dict(
        g_mix=g_mix_norm[0][None], wu=w[:, :s1].astype(BF16), wz=w[:, s1:s2].astype(BF16),
        wx=w[:, s2:s3].astype(BF16), wdt=wdt.astype(BF16), cs=_channel_dft_table(),
        conv_w=conv_w[0, :, 0, :], conv_b=conv_b[0][None],
        tri=jnp.asarray(np.stack([lower, lower.T])), alog=pad_heads(a_log_fwd, a_log_bwd),
        dtb=pad_heads(dt_bias_fwd, dt_bias_bwd), dsk=dsk,
        g_four=g_fourier_norm[0][None], g_ssd=g_ssd_norm[0][None],
        w4=w_out[0, :FOURIER_W].astype(BF16), wy=w_out[0, FOURIER_W:].astype(BF16),
        g_x=g_xattn_norm[0][None], g_mem=g_mem_norm[0][None],
        wq=w_q[0].astype(BF16), wk=w_k[0].astype(BF16), wv=w_v[0].astype(BF16), wo=w_o[0].astype(BF16),
        g_moe=g_moe_norm[0][None], wr_t=w_router[0].T, br=b_router[0][:, None],
        w1=w_mlp1[0].astype(BF16), b1=b_mlp1[0][:, None, :], w2=w_mlp2[0].astype(BF16),
        b2=b_mlp2[0][:, None, :], g_final=g_final[None])


def _encode_group(x, mem, p):
    bsz, seqlen, _ = x.shape
    t = bsz * seqlen
    tm = min(512, seqlen)
    x2d = x.reshape(t, D_MODEL)

    ab, z, xbc, dt = _inproj(x2d, p["g_mix"], p["wu"], p["wz"], p["wx"], p["wdt"], p["cs"], tm)
    cos_t, nsin_t = _dft_tables(seqlen)
    nb = 4 if bsz % 4 == 0 else (2 if bsz % 2 == 0 else 1)
    tf = min(1024, seqlen)
    four = _fourier(ab.reshape(bsz, seqlen, 2 * FOURIER_W), cos_t, nsin_t, p["g_four"], nb, tf, tf)
    xbc = _conv(xbc.reshape(bsz, seqlen, CONV_CH), p["conv_w"], p["conv_b"], tm, 512)
    y2 = _ssd(xbc, dt.reshape(bsz, seqlen, 2 * LANES), p["tri"], p["alog"], p["dtb"], p["dsk"])
    h = _outproj(x2d, four.reshape(t, FOURIER_W), y2.reshape(2, t, SSD_W), z, p["g_ssd"],
                 p["w4"], p["wy"], tm)

    k, v = _kv(mem, p["g_mem"], p["wk"], p["wv"])
    h = _xattn(h.reshape(bsz, seqlen, D_MODEL), p["g_x"], p["wq"], k, v, p["wo"], tm).reshape(t, D_MODEL)

    xn, idx, gate, rank, cnt = _router(h, p["g_moe"], p["wr_t"], p["br"], tm)
    counts = cnt[:, 0]
    padded = ((counts + MOE_BLK - 1) // MOE_BLK) * MOE_BLK
    pend = jnp.cumsum(padded)
    pstart = (pend - padded).astype(jnp.int32)
    n_blocks = (t * TOP_K) // MOE_BLK + N_EXPERTS
    n_slots = n_blocks * MOE_BLK
    n_active = (pend[-1] // MOE_BLK).astype(jnp.int32)
    blk = jnp.arange(n_blocks, dtype=jnp.int32)
    be = jnp.minimum(jnp.searchsorted(pend, blk * MOE_BLK, side="right"), N_EXPERTS - 1).astype(jnp.int32)
    be = jnp.where(blk < n_active, be, be[jnp.maximum(n_active - 1, 0)])
    last_blk = (pend - MOE_BLK).astype(jnp.int32)[jnp.argsort(padded == 0)]
    n_nonempty = jnp.sum(padded > 0).astype(jnp.int32)
    tail = (n_active + jnp.arange(N_EXPERTS, dtype=jnp.int32)) * MOE_BLK
    j = jnp.arange(2 * N_EXPERTS, dtype=jnp.int32)
    zstart = jnp.where(j < n_nonempty, last_blk[jnp.minimum(j, N_EXPERTS - 1)],
                       tail[jnp.clip(j - n_nonempty, 0, N_EXPERTS - 1)]).astype(jnp.int32)
    nzero = (n_nonempty + (n_blocks - n_active)).astype(jnp.int32)[None]

    td = min(256, t)
    idx_t = _tile_rows(idx, td)
    rank_t = _tile_rows(rank, td)
    xs = _dispatch(pstart, zstart, nzero, idx_t, rank_t, xn, n_slots, td)
    outs = _experts(be, n_active[None], xs, p["w1"], p["b1"], p["w2"], p["b2"])
    y = _combine(pstart, idx_t, rank_t, h, gate, p["g_final"], outs, td)
    return y.reshape(bsz, seqlen, D_MODEL)


def kernel(x_prompt, x_sample, mem_prompt, mem_sample, g_mix_norm, w_in, conv_w, conv_b, a_log_fwd, a_log_bwd, dt_bias_fwd, dt_bias_bwd, d_skip, g_fourier_norm, g_ssd_norm, w_out, g_xattn_norm, g_mem_norm, w_q, w_k, w_v, w_o, g_moe_norm, w_router, b_router, w_mlp1, b_mlp1, w_mlp2, b_mlp2, g_final):
    p = _prep_weights(g_mix_norm, w_in, conv_w, conv_b, a_log_fwd, a_log_bwd, dt_bias_fwd,
                      dt_bias_bwd, d_skip, g_fourier_norm, g_ssd_norm, w_out, g_xattn_norm,
                      g_mem_norm, w_q, w_k, w_v, w_o, g_moe_norm, w_router, b_router, w_mlp1,
                      b_mlp1, w_mlp2, b_mlp2, g_final)
    return (_encode_group(x_prompt, mem_prompt, p), _encode_group(x_sample, mem_sample, p))
```

```python
import functools

import numpy as np
import jax
import jax.numpy as jnp
from jax import lax
from jax.experimental import pallas as pl
from jax.experimental.pallas import tpu as pltpu

F32 = jnp.float32
BF16 = jnp.bfloat16

D_MODEL = 1024
D_MIX = 2 * D_MODEL
FOURIER_W = D_MIX // 4
FOURIER_GROUPS = 8
FOURIER_GROUP_DIM = FOURIER_W // FOURIER_GROUPS
SSD_W = D_MIX - FOURIER_W
SSD_HEAD_DIM = 64
SSD_HEADS = SSD_W // SSD_HEAD_DIM
SSD_GROUPS = 4
SSD_HPG = SSD_HEADS // SSD_GROUPS
D_STATE = 128
CONV_WIDTH = 5
CONV_PAD = CONV_WIDTH // 2
GN = SSD_GROUPS * D_STATE
CONV_CH = SSD_W + 2 * GN
CHUNK = 128
XATTN_HEADS = 4
XATTN_HEAD_DIM = D_MODEL // XATTN_HEADS
N_EXPERTS = 32
TOP_K = 4
D_FF = D_MODEL
SWIGLU_ALPHA = 1.702
SWIGLU_LIMIT = 7.0
NORM_EPS = 1e-5

LANES = 128
HALO = 16
MOE_BLK = 512
MOE_TILE = 256
RUN_ALIGN = 8
VMEM_LIMIT = 48 * 1024 * 1024


def _cparams(sem):
    return pltpu.CompilerParams(dimension_semantics=sem, vmem_limit_bytes=VMEM_LIMIT)


def _rms(x, g):
    return x * lax.rsqrt(jnp.mean(x * x, axis=-1, keepdims=True) + NORM_EPS) * g


def _sigmoid(x):
    return 1.0 / (1.0 + jnp.exp(-x))


def _dot(a, b):
    return jnp.dot(a, b, preferred_element_type=F32)


def _dot_nt(a, b):
    return lax.dot_general(a, b, (((1,), (1,)), ((), ())), preferred_element_type=F32)


def _split3(x):
    h = x.astype(BF16)
    r = x - h.astype(F32)
    m = r.astype(BF16)
    l = (r - m.astype(F32)).astype(BF16)
    return h, m, l


def _inproj_kernel(x_ref, g_ref, wu_ref, wz_ref, wx_ref, wdt_ref, cs_ref,
                   ab_ref, z_ref, xbc_ref, dt_ref):
    xn = _rms(x_ref[...], g_ref[...]).astype(BF16)
    u = _dot(xn, wu_ref[...])
    ab_ref[...] = _dot(u.astype(BF16), cs_ref[...]).astype(BF16)
    z_ref[...] = _dot(xn, wz_ref[...]).astype(BF16)
    xbc_ref[...] = _dot(xn, wx_ref[...]).astype(BF16)
    dt_ref[...] = _dot(xn, wdt_ref[...])


def _inproj(x2d, g, wu, wz, wx, wdt, cs, tm):
    t = x2d.shape[0]
    full = lambda a: pl.BlockSpec(a.shape, lambda i: (0,) * a.ndim)
    row = lambda w: pl.BlockSpec((tm, w), lambda i: (i, 0))
    return pl.pallas_call(
        _inproj_kernel,
        grid=(t // tm,),
        in_specs=[row(D_MODEL), full(g), full(wu), full(wz), full(wx), full(wdt), full(cs)],
        out_specs=[row(2 * FOURIER_W), row(SSD_W), row(CONV_CH), row(2 * LANES)],
        out_shape=[jax.ShapeDtypeStruct((t, 2 * FOURIER_W), BF16),
                   jax.ShapeDtypeStruct((t, SSD_W), BF16),
                   jax.ShapeDtypeStruct((t, CONV_CH), BF16),
                   jax.ShapeDtypeStruct((t, 2 * LANES), F32)],
        compiler_params=_cparams(("parallel",)),
        name="inproj",
    )(x2d, g, wu, wz, wx, wdt, cs)


def _fourier_kernel(c_ref, s_ref, ab_ref, g_ref, o_ref, acc_ref, *, scale, nb):
    k = pl.program_id(2)

    @pl.when(k == 0)
    def _():
        acc_ref[...] = jnp.zeros_like(acc_ref)

    c = c_ref[...]
    s = s_ref[...]
    for b in range(nb):
        acc_ref[b] += _dot(c, ab_ref[b, :, :FOURIER_W]) + _dot(s, ab_ref[b, :, FOURIER_W:])

    @pl.when(k == pl.num_programs(2) - 1)
    def _():
        for b in range(nb):
            o_ref[b] = _rms(acc_ref[b] * scale, g_ref[...]).astype(BF16)


def _fourier(ab, cos_t, nsin_t, g, nb, tr, tk):
    bsz, seqlen, _ = ab.shape
    scale = float(1.0 / np.sqrt(seqlen * FOURIER_GROUP_DIM))
    return pl.pallas_call(
        functools.partial(_fourier_kernel, scale=scale, nb=nb),
        grid=(bsz // nb, seqlen // tr, seqlen // tk),
        in_specs=[pl.BlockSpec((tr, tk), lambda b, i, k: (i, k)),
                  pl.BlockSpec((tr, tk), lambda b, i, k: (i, k)),
                  pl.BlockSpec((nb, tk, 2 * FOURIER_W), lambda b, i, k: (b, k, 0)),
                  pl.BlockSpec((1, FOURIER_W), lambda b, i, k: (0, 0))],
        out_specs=pl.BlockSpec((nb, tr, FOURIER_W), lambda b, i, k: (b, i, 0)),
        out_shape=jax.ShapeDtypeStruct((bsz, seqlen, FOURIER_W), BF16),
        scratch_shapes=[pltpu.VMEM((nb, tr, FOURIER_W), F32)],
        compiler_params=_cparams(("parallel", "parallel", "arbitrary")),
        name="fourier",
    )(cos_t, nsin_t, ab, g)


def _dft_tables(seqlen):
    lo = min(seqlen, LANES)
    hi = seqlen // lo
    k = np.arange(seqlen, dtype=np.int64)
    a_hi = 2.0 * np.pi * ((np.arange(hi, dtype=np.int64)[:, None] * lo * k[None, :]) % seqlen) / seqlen
    a_lo = 2.0 * np.pi * ((np.arange(lo, dtype=np.int64)[:, None] * k[None, :]) % seqlen) / seqlen
    c1 = jnp.asarray(np.cos(a_hi), F32)[:, None, :]
    s1 = jnp.asarray(np.sin(a_hi), F32)[:, None, :]
    c0 = jnp.asarray(np.cos(a_lo), F32)[None, :, :]
    s0 = jnp.asarray(np.sin(a_lo), F32)[None, :, :]
    cos_t = (c1 * c0 - s1 * s0).reshape(seqlen, seqlen).astype(BF16)
    nsin_t = (-(s1 * c0 + c1 * s0)).reshape(seqlen, seqlen).astype(BF16)
    return cos_t, nsin_t


def _channel_dft_table():
    c = np.arange(FOURIER_W)
    same = (c[:, None] // FOURIER_GROUP_DIM) == (c[None, :] // FOURIER_GROUP_DIM)
    ang = 2.0 * np.pi * ((c[:, None] % FOURIER_GROUP_DIM) * (c[None, :] % FOURIER_GROUP_DIM)
                         % FOURIER_GROUP_DIM) / FOURIER_GROUP_DIM
    cc = np.where(same, np.cos(ang), 0.0)
    sc = np.where(same, np.sin(ang), 0.0)
    return jnp.asarray(np.concatenate([cc, sc], axis=1), BF16)


def _conv_kernel(prev_ref, main_ref, next_ref, w_ref, b_ref, o_ref):
    i = pl.program_id(1)
    tc = main_ref.shape[1]
    prev = jnp.where(i > 0, prev_ref[0].astype(F32), 0.0)
    nxt = jnp.where(i < pl.num_programs(1) - 1, next_ref[0].astype(F32), 0.0)
    xe = jnp.concatenate([prev, main_ref[0].astype(F32), nxt], axis=0)
    rows = tc + 2 * HALO
    acc = b_ref[...] + w_ref[CONV_PAD:CONV_PAD + 1, :] * xe
    for w in range(CONV_WIDTH):
        if w == CONV_PAD:
            continue
        acc = acc + w_ref[w:w + 1, :] * pltpu.roll(xe, (CONV_PAD - w) % rows, axis=0)
    y = acc[HALO:HALO + tc]
    o_ref[0] = (y * _sigmoid(y)).astype(BF16)


def _conv(xbc, w, b, tc, cw):
    bsz, seqlen, _ = xbc.shape
    r = tc // HALO
    nh = seqlen // HALO
    return pl.pallas_call(
        _conv_kernel,
        grid=(bsz, seqlen // tc, CONV_CH // cw),
        in_specs=[pl.BlockSpec((1, HALO, cw), lambda bi, i, c: (bi, jnp.maximum(i * r - 1, 0), c)),
                  pl.BlockSpec((1, tc, cw), lambda bi, i, c: (bi, i, c)),
                  pl.BlockSpec((1, HALO, cw), lambda bi, i, c: (bi, jnp.minimum((i + 1) * r, nh - 1), c)),
                  pl.BlockSpec((CONV_WIDTH, cw), lambda bi, i, c: (0, c)),
                  pl.BlockSpec((1, cw), lambda bi, i, c: (0, c))],
        out_specs=pl.BlockSpec((1, tc, cw), lambda bi, i, c: (bi, i, c)),
        out_shape=jax.ShapeDtypeStruct(xbc.shape, BF16),
        compiler_params=_cparams(("parallel", "parallel", "parallel")),
        name="conv",
    )(xbc, xbc, xbc, w, b)


def _ssd_kernel(xbc_ref, dt_ref, tri_ref, alog_ref, dtb_ref, dsk_ref, y_ref, state_ref):
    q = CHUNK

    @pl.when(pl.program_id(2) == 0)
    def _():
        state_ref[...] = jnp.zeros_like(state_ref)

    tri = tri_ref[0]
    mask = tri > 0.5
    a = -jnp.exp(alog_ref[0])
    dtr = dt_ref[0] + dtb_ref[0]
    dt = jnp.maximum(dtr, 0.0) + jnp.log1p(jnp.exp(-jnp.abs(dtr)))
    da = dt * a
    tri_b = tri.astype(BF16)
    d0, d1, d2 = _split3(da)
    acum = _dot(tri_b, d0) + _dot(tri_b, d1) + _dot(tri_b, d2)
    total = jnp.sum(da, axis=0, keepdims=True)
    eacum = jnp.exp(acum)
    etot = jnp.exp(total)
    acum_t = acum.T
    dt_t = dt.T
    w_t = (dt * jnp.exp(total - acum)).T
    lane_lo = lax.broadcasted_iota(jnp.int32, (1, LANES), 1) < SSD_HEAD_DIM
    zero_b = jnp.zeros((), BF16)

    for g in range(SSD_GROUPS):
        bm = xbc_ref[0, :, SSD_W + g * D_STATE:SSD_W + (g + 1) * D_STATE]
        cm = xbc_ref[0, :, SSD_W + GN + g * D_STATE:SSD_W + GN + (g + 1) * D_STATE]
        cb = _dot_nt(cm, bm)
        cm_f = cm.astype(F32)
        bm_t = bm.astype(F32).T
        for p in range(SSD_HPG // 2):
            t = g * (SSD_HPG // 2) + p
            cols = slice(t * LANES, (t + 1) * LANES)
            xs = xbc_ref[0, :, cols]
            st = state_ref[:, cols]
            xs_lo = jnp.where(lane_lo, xs, zero_b)
            xs_hi = jnp.where(lane_lo, zero_b, xs)
            st_b = st.astype(BF16)
            st_lo = jnp.where(lane_lo, st_b, zero_b)
            st_hi = jnp.where(lane_lo, zero_b, st_b)
            lhs = []
            lhs2 = []
            for h in (2 * t, 2 * t + 1):
                colb = jnp.broadcast_to(acum[:, h:h + 1], (q, q))
                decay = jnp.where(mask, jnp.exp(colb - acum_t[h:h + 1, :]), 0.0)
                lhs.append((cb * decay * dt_t[h:h + 1, :]).astype(BF16))
                lhs.append((cm_f * jnp.broadcast_to(eacum[:, h:h + 1], (q, D_STATE))).astype(BF16))
                lhs2.append((bm_t * w_t[h:h + 1, :]).astype(BF16))
            y = _dot(jnp.concatenate(lhs, axis=1),
                     jnp.concatenate([xs_lo, st_lo, xs_hi, st_hi], axis=0))
            y = y + dsk_ref[0, :, cols] * xs.astype(F32)
            y_ref[0, 0, :, cols] = y.astype(BF16)
            dec = jnp.where(lane_lo, etot[:, 2 * t:2 * t + 1], etot[:, 2 * t + 1:2 * t + 2])
            state_ref[:, cols] = st * dec + _dot(jnp.concatenate(lhs2, axis=1),
                                                 jnp.concatenate([xs_lo, xs_hi], axis=0))


def _ssd(xbc, dt, tri, alog, dtb, dsk):
    bsz, seqlen, _ = xbc.shape
    nc = seqlen // CHUNK
    chunk = lambda d, j: j + d * (nc - 1 - 2 * j)
    return pl.pallas_call(
        _ssd_kernel,
        grid=(bsz, 2, nc),
        in_specs=[pl.BlockSpec((1, CHUNK, CONV_CH), lambda b, d, j: (b, chunk(d, j), 0)),
                  pl.BlockSpec((1, CHUNK, LANES), lambda b, d, j: (b, chunk(d, j), d)),
                  pl.BlockSpec((1, CHUNK, CHUNK), lambda b, d, j: (d, 0, 0)),
                  pl.BlockSpec((1, 1, LANES), lambda b, d, j: (d, 0, 0)),
                  pl.BlockSpec((1, 1, LANES), lambda b, d, j: (d, 0, 0)),
                  pl.BlockSpec((1, 1, SSD_W), lambda b, d, j: (d, 0, 0))],
        out_specs=pl.BlockSpec((1, 1, CHUNK, SSD_W), lambda b, d, j: (d, b, chunk(d, j), 0)),
        out_shape=jax.ShapeDtypeStruct((2, bsz, seqlen, SSD_W), BF16),
        scratch_shapes=[pltpu.VMEM((D_STATE, SSD_W), F32)],
        compiler_params=_cparams(("parallel", "arbitrary", "arbitrary")),
        name="ssd",
    )(xbc, dt, tri, alog, dtb, dsk)


def _outproj_kernel(x_ref, four_ref, y_ref, z_ref, gs_ref, w4_ref, wy_ref, o_ref):
    y = y_ref[0].astype(F32) + y_ref[1].astype(F32)
    z = z_ref[...].astype(F32)
    y = y * (z * _sigmoid(z))
    yn = _rms(y, gs_ref[...]).astype(BF16)
    o_ref[...] = x_ref[...] + _dot(four_ref[...], w4_ref[...]) + _dot(yn, wy_ref[...])


def _outproj(x2d, four2d, y2, z2d, gs, w4, wy, tm):
    t = x2d.shape[0]
    full = lambda a: pl.BlockSpec(a.shape, lambda i: (0,) * a.ndim)
    row = lambda w: pl.BlockSpec((tm, w), lambda i: (i, 0))
    return pl.pallas_call(
        _outproj_kernel,
        grid=(t // tm,),
        in_specs=[row(D_MODEL), row(FOURIER_W), pl.BlockSpec((2, tm, SSD_W), lambda i: (0, i, 0)),
                  row(SSD_W), full(gs), full(w4), full(wy)],
        out_specs=row(D_MODEL),
        out_shape=jax.ShapeDtypeStruct((t, D_MODEL), F32),
        compiler_params=_cparams(("parallel",)),
        name="outproj",
    )(x2d, four2d, y2, z2d, gs, w4, wy)


def _kv_kernel(mem_ref, g_ref, wk_ref, wv_ref, k_ref, v_ref):
    mn = _rms(mem_ref[0], g_ref[...]).astype(BF16)
    k_ref[0] = _dot(mn, wk_ref[...]).astype(BF16)
    v_ref[0] = _dot(mn, wv_ref[...]).astype(BF16)


def _kv(mem, g, wk, wv):
    bsz, nm, _ = mem.shape
    full = lambda a: pl.BlockSpec(a.shape, lambda b: (0,) * a.ndim)
    blk = pl.BlockSpec((1, nm, D_MODEL), lambda b: (b, 0, 0))
    return pl.pallas_call(
        _kv_kernel,
        grid=(bsz,),
        in_specs=[blk, full(g), full(wk), full(wv)],
        out_specs=[blk, blk],
        out_shape=[jax.ShapeDtypeStruct(mem.shape, BF16)] * 2,
        compiler_params=_cparams(("parallel",)),
        name="kv",
    )(mem, g, wk, wv)


def _xattn_kernel(h_ref, g_ref, wq_ref, k_ref, v_ref, wo_ref, o_ref):
    h = h_ref[0]
    hn = _rms(h, g_ref[...]).astype(BF16)
    q = (_dot(hn, wq_ref[...]) * (XATTN_HEAD_DIM ** -0.5)).astype(BF16)
    outs = []
    for hd in range(XATTN_HEADS):
        cols = slice(hd * XATTN_HEAD_DIM, (hd + 1) * XATTN_HEAD_DIM)
        s = _dot_nt(q[:, cols], k_ref[0, :, cols])
        e = jnp.exp(s - jnp.max(s, axis=-1, keepdims=True))
        p = (e / jnp.sum(e, axis=-1, keepdims=True)).astype(BF16)
        outs.append(_dot(p, v_ref[0, :, cols]).astype(BF16))
    o_ref[0] = h + _dot(jnp.concatenate(outs, axis=1), wo_ref[...])


def _xattn(h, g, wq, k, v, wo, tm):
    bsz, seqlen, _ = h.shape
    nm = k.shape[1]
    full = lambda a: pl.BlockSpec(a.shape, lambda b, i: (0,) * a.ndim)
    tok = pl.BlockSpec((1, tm, D_MODEL), lambda b, i: (b, i, 0))
    mem = pl.BlockSpec((1, nm, D_MODEL), lambda b, i: (b, 0, 0))
    return pl.pallas_call(
        _xattn_kernel,
        grid=(bsz, seqlen // tm),
        in_specs=[tok, full(g), full(wq), mem, mem, full(wo)],
        out_specs=tok,
        out_shape=jax.ShapeDtypeStruct(h.shape, F32),
        compiler_params=_cparams(("parallel", "parallel")),
        name="xattn",
    )(h, g, wq, k, v, wo)


def _router_kernel(h_ref, g_ref, wr_ref, br_ref, xn_ref, rloc_ref, gate_ref, cnt_ref, car_ref,
                   carry_ref, cnt_acc, car_acc):
    i = pl.program_id(0)
    tm = h_ref.shape[0]

    @pl.when(i == 0)
    def _():
        carry_ref[...] = jnp.zeros_like(carry_ref)
        cnt_acc[...] = jnp.zeros_like(cnt_acc)
        car_acc[...] = jnp.zeros_like(car_acc)

    xn = _rms(h_ref[...], g_ref[...])
    xn_ref[...] = xn.astype(BF16)
    x0, x1, _ = _split3(xn)
    w0, w1, _ = _split3(wr_ref[...])
    logits = _dot_nt(w0, x0) + _dot_nt(w0, x1) + _dot_nt(w1, x0) + br_ref[...]

    eid = lax.broadcasted_iota(jnp.int32, (N_EXPERTS, tm), 0)
    work = logits
    vals, hits = [], []
    sel = jnp.zeros((N_EXPERTS, tm), F32)
    for k in range(TOP_K):
        m = jnp.max(work, axis=0, keepdims=True)
        ik = jnp.min(jnp.where(work == m, eid, N_EXPERTS), axis=0, keepdims=True)
        hit = eid == ik
        vals.append(m)
        hits.append(hit)
        sel = jnp.where(hit, 1.0, sel)
        work = jnp.where(hit, -jnp.inf, work)
    es = [jnp.exp(v - vals[0]) for v in vals]
    den = es[0] + es[1] + es[2] + es[3]
    for k in range(TOP_K):
        gate_ref[k:k + 1, :] = es[k] / den

    r = lax.broadcasted_iota(jnp.int32, (tm, tm), 0)
    c = lax.broadcasted_iota(jnp.int32, (tm, tm), 1)
    before = jnp.where(r < c, 1.0, 0.0).astype(BF16)
    cum = _dot(sel.astype(BF16), before)
    counts = jnp.sum(sel, axis=1, keepdims=True)
    counts = jnp.floor((counts + (RUN_ALIGN - 1)) * (1.0 / RUN_ALIGN)) * RUN_ALIGN
    er = lax.broadcasted_iota(jnp.int32, (N_EXPERTS, N_EXPERTS), 0)
    ec = lax.broadcasted_iota(jnp.int32, (N_EXPERTS, N_EXPERTS), 1)
    lower = jnp.where(er > ec, 1.0, 0.0).astype(BF16)
    offs = _dot(lower, jnp.broadcast_to(counts, (N_EXPERTS, LANES)).astype(BF16))[:, 0:1]
    pos = cum + offs
    for k in range(TOP_K):
        rk = jnp.sum(jnp.where(hits[k], pos, 0.0), axis=0, keepdims=True)
        rloc_ref[k:k + 1, :] = rk.astype(jnp.int32)

    lane = lax.broadcasted_iota(jnp.int32, cnt_acc.shape, 1)
    cnt_acc[...] = jnp.where(lane == i, counts, cnt_acc[...])
    car_acc[...] = jnp.where(lane == i, carry_ref[:, 0:1], car_acc[...])
    carry_ref[...] = carry_ref[...] + counts
    cnt_ref[...] = cnt_acc[...].astype(jnp.int32)
    car_ref[...] = car_acc[...].astype(jnp.int32)


def _router(h2d, g, wr_t, br, tm):
    t = h2d.shape[0]
    nt = t // tm
    full = lambda a: pl.BlockSpec(a.shape, lambda i: (0,) * a.ndim)
    kt = pl.BlockSpec((TOP_K, tm), lambda i: (0, i))
    per_tile = pl.BlockSpec((N_EXPERTS, nt), lambda i: (0, 0))
    return pl.pallas_call(
        _router_kernel,
        grid=(nt,),
        in_specs=[pl.BlockSpec((tm, D_MODEL), lambda i: (i, 0)), full(g), full(wr_t), full(br)],
        out_specs=[pl.BlockSpec((tm, D_MODEL), lambda i: (i, 0)), kt, kt, per_tile, per_tile],
        out_shape=[jax.ShapeDtypeStruct((t, D_MODEL), BF16),
                   jax.ShapeDtypeStruct((TOP_K, t), jnp.int32),
                   jax.ShapeDtypeStruct((TOP_K, t), F32),
                   jax.ShapeDtypeStruct((N_EXPERTS, nt), jnp.int32),
                   jax.ShapeDtypeStruct((N_EXPERTS, nt), jnp.int32)],
        scratch_shapes=[pltpu.VMEM((N_EXPERTS, LANES), F32),
                        pltpu.VMEM((N_EXPERTS, nt), F32),
                        pltpu.VMEM((N_EXPERTS, nt), F32)],
        compiler_params=_cparams(("arbitrary",)),
        name="router",
    )(h2d, g, wr_t, br)


def _pow2_pieces(n, max_size, piece_fn):
    done = jnp.int32(0)
    size = max_size
    while size >= RUN_ALIGN:
        bit = n & size

        @pl.when(bit != 0)
        def _(size=size, done=done):
            piece_fn(done, size)

        done = done + bit
        size //= 2


def _for_each_run(tile, cnt_ref, car_ref, pstart_ref, tile_rows, piece_fn):
    def per_expert(e, local):
        n = cnt_ref[tile * N_EXPERTS + e]
        local = pl.multiple_of(local, RUN_ALIGN)
        slot = pl.multiple_of(pstart_ref[e] + car_ref[tile * N_EXPERTS + e], RUN_ALIGN)
        _pow2_pieces(n, tile_rows, lambda done, size: piece_fn(
            pl.multiple_of(local + done, RUN_ALIGN), pl.multiple_of(slot + done, RUN_ALIGN), size))
        return local + n

    return lax.fori_loop(0, N_EXPERTS, per_expert, jnp.int32(0))


def _sorted_rows(tile_rows):
    return TOP_K * tile_rows + N_EXPERTS * RUN_ALIGN


def _sorted_onehot(rloc_ref, tile_rows):
    rows = lax.broadcasted_iota(jnp.int32, (_sorted_rows(tile_rows), tile_rows), 0)
    return [rows == rloc_ref[k:k + 1, :] for k in range(TOP_K)]


def _dispatch_kernel(pstart_ref, cnt_ref, car_ref, zstart_ref, nzero_ref, x_ref, rloc_ref, xs_hbm,
                     ybuf, zbuf, total_s, sem_z, sem_y):
    i = pl.program_id(0)
    td = x_ref.shape[0]
    slot = i % 2
    half = D_MODEL // 2

    def wait_slot(s):
        def wait_piece(done, size):
            del done
            pltpu.make_async_copy(ybuf.at[s, pl.ds(0, size)], xs_hbm.at[pl.ds(0, size)],
                                  sem_y.at[s]).wait()

        _pow2_pieces(total_s[s], TOP_K * td, wait_piece)

    @pl.when(i == 0)
    def _():
        zbuf[...] = jnp.zeros_like(zbuf)

        def zcopy(start):
            dst = pl.ds(pl.multiple_of(start, MOE_BLK), MOE_BLK)
            return pltpu.make_async_copy(zbuf, xs_hbm.at[dst], sem_z)

        def issue(j, carry):
            zcopy(zstart_ref[j]).start()
            return carry

        def issue_tail(j, carry):
            zcopy(j * MOE_BLK).start()
            return carry

        def drain(j, carry):
            zcopy(0).wait()
            return carry

        n_blocks = xs_hbm.shape[0] // MOE_BLK
        lax.fori_loop(0, nzero_ref[0], issue, 0)
        lax.fori_loop(nzero_ref[1], n_blocks, issue_tail, 0)
        lax.fori_loop(0, nzero_ref[0] + (n_blocks - nzero_ref[1]), drain, 0)

    @pl.when(i >= 2)
    def _():
        wait_slot(slot)

    onehot = jnp.zeros((_sorted_rows(td), td), F32)
    for hit in _sorted_onehot(rloc_ref, td):
        onehot = jnp.where(hit, 1.0, onehot)
    y = lax.bitcast_convert_type(_dot(onehot.astype(BF16), x_ref[...]), jnp.uint32)
    ybuf[slot] = (y[:, :half] >> 16) | (y[:, half:] & jnp.uint32(0xFFFF0000))

    def send(local, dst, size):
        pltpu.make_async_copy(ybuf.at[slot, pl.ds(local, size)], xs_hbm.at[pl.ds(dst, size)],
                              sem_y.at[slot]).start()

    total_s[slot] = _for_each_run(i, cnt_ref, car_ref, pstart_ref, td, send)

    @pl.when(i == pl.num_programs(0) - 1)
    def _():
        wait_slot(slot)

        @pl.when(i >= 1)
        def _():
            wait_slot(1 - slot)


def _dispatch(pstart, cnt, car, zstart, nzero, xn, rloc, n_slots, td):
    t = xn.shape[0]
    half = D_MODEL // 2
    return pl.pallas_call(
        _dispatch_kernel,
        grid_spec=pltpu.PrefetchScalarGridSpec(
            num_scalar_prefetch=5,
            grid=(t // td,),
            in_specs=[pl.BlockSpec((td, D_MODEL), lambda i, *_: (i, 0)),
                      pl.BlockSpec((TOP_K, td), lambda i, *_: (0, i))],
            out_specs=pl.BlockSpec(memory_space=pl.ANY),
            scratch_shapes=[pltpu.VMEM((2, _sorted_rows(td), half), jnp.uint32),
                            pltpu.VMEM((MOE_BLK, half), jnp.uint32),
                            pltpu.SMEM((2,), jnp.int32),
                            pltpu.SemaphoreType.DMA,
                            pltpu.SemaphoreType.DMA((2,))]),
        out_shape=jax.ShapeDtypeStruct((n_slots, half), jnp.uint32),
        compiler_params=_cparams(("arbitrary",)),
        name="dispatch",
    )(pstart, cnt, car, zstart, nzero, xn, rloc)


def _expert_kernel(be_ref, na_ref, xs_ref, w1_ref, b1_ref, w2_ref, b2_ref, o_ref):
    i = pl.program_id(0)

    @pl.when(i < na_ref[0])
    def _():
        w = xs_ref[...]
        lo = lax.bitcast_convert_type(w << 16, F32)
        hi = lax.bitcast_convert_type(w & jnp.uint32(0xFFFF0000), F32)
        xb = jnp.concatenate([lo, hi], axis=1).astype(BF16)
        h = _dot(xb, w1_ref[0]) + b1_ref[0]
        glu = jnp.minimum(h[:, :D_FF], SWIGLU_LIMIT)
        lin = jnp.clip(h[:, D_FF:], -SWIGLU_LIMIT, SWIGLU_LIMIT)
        act = glu * _sigmoid(SWIGLU_ALPHA * glu) * (lin + 1.0)
        o_ref[...] = _dot(act.astype(BF16), w2_ref[0]) + b2_ref[0]

    @pl.when(i >= na_ref[0])
    def _():
        o_ref[...] = jnp.zeros_like(o_ref)


def _experts(block_e, n_active, xs, w1, b1, w2, b2):
    n_slots = xs.shape[0]
    return pl.pallas_call(
        _expert_kernel,
        grid_spec=pltpu.PrefetchScalarGridSpec(
            num_scalar_prefetch=2,
            grid=(n_slots // MOE_BLK,),
            in_specs=[pl.BlockSpec((MOE_BLK, D_MODEL // 2),
                                   lambda i, be, na: (jnp.minimum(i, na[0] - 1), 0)),
                      pl.BlockSpec((1, D_MODEL, 2 * D_FF), lambda i, be, na: (be[i], 0, 0)),
                      pl.BlockSpec((1, 1, 2 * D_FF), lambda i, be, na: (be[i], 0, 0)),
                      pl.BlockSpec((1, D_FF, D_MODEL), lambda i, be, na: (be[i], 0, 0)),
                      pl.BlockSpec((1, 1, D_MODEL), lambda i, be, na: (be[i], 0, 0))],
            out_specs=pl.BlockSpec((MOE_BLK, D_MODEL), lambda i, be, na: (i, 0))),
        out_shape=jax.ShapeDtypeStruct((n_slots, D_MODEL), F32),
        compiler_params=_cparams(("arbitrary",)),
        name="experts",
    )(block_e, n_active, xs, w1, b1, w2, b2)


def _combine_kernel(pstart_ref, cnt_ref, car_ref, h_ref, rloc_ref, gate_ref, gf_ref, outs_hbm, o_ref,
                    ybuf, total_s, sem_y):
    i = pl.program_id(0)
    nt = pl.num_programs(0)
    tc = h_ref.shape[0]
    slot = i % 2

    def fetch(tile, s):
        def piece(local, src, size):
            pltpu.make_async_copy(outs_hbm.at[pl.ds(src, size)], ybuf.at[s, pl.ds(local, size)],
                                  sem_y.at[s]).start()

        total_s[s] = _for_each_run(tile, cnt_ref, car_ref, pstart_ref, tc, piece)

    @pl.when(i == 0)
    def _():
        fetch(i, slot)

    @pl.when(i + 1 < nt)
    def _():
        fetch(i + 1, 1 - slot)

    hits = _sorted_onehot(rloc_ref, tc)
    gsel = jnp.zeros((_sorted_rows(tc), tc), F32)
    for k in range(TOP_K):
        gsel = jnp.where(hits[k], gate_ref[k:k + 1, :], gsel)
    g_t = gsel.T
    g0 = g_t.astype(BF16)
    g1 = (g_t - g0.astype(F32)).astype(BF16)

    def wait_piece(done, size):
        del done
        pltpu.make_async_copy(outs_hbm.at[pl.ds(0, size)], ybuf.at[slot, pl.ds(0, size)],
                              sem_y.at[slot]).wait()

    _pow2_pieces(total_s[slot], TOP_K * tc, wait_piece)
    row = lax.broadcasted_iota(jnp.int32, (_sorted_rows(tc), 1), 0)
    yb = jnp.where(row < total_s[slot], ybuf[slot], 0.0).astype(BF16)
    o_ref[...] = _rms(h_ref[...] + _dot(g0, yb) + _dot(g1, yb), gf_ref[...])


def _combine(pstart, cnt, car, h2d, rloc, gate, gf, outs, tc):
    t = h2d.shape[0]
    kt = pl.BlockSpec((TOP_K, tc), lambda i, *_: (0, i))
    return pl.pallas_call(
        _combine_kernel,
        grid_spec=pltpu.PrefetchScalarGridSpec(
            num_scalar_prefetch=3,
            grid=(t // tc,),
            in_specs=[pl.BlockSpec((tc, D_MODEL), lambda i, *_: (i, 0)), kt, kt,
                      pl.BlockSpec((1, D_MODEL), lambda i, *_: (0, 0)),
                      pl.BlockSpec(memory_space=pl.ANY)],
            out_specs=pl.BlockSpec((tc, D_MODEL), lambda i, *_: (i, 0)),
            scratch_shapes=[pltpu.VMEM((2, _sorted_rows(tc), D_MODEL), F32),
                            pltpu.SMEM((2,), jnp.int32),
                            pltpu.SemaphoreType.DMA((2,))]),
        out_shape=jax.ShapeDtypeStruct((t, D_MODEL), F32),
        compiler_params=_cparams(("arbitrary",)),
        name="combine",
    )(pstart, cnt, car, h2d, rloc, gate, gf, outs)


def _prep_weights(g_mix_norm, w_in, conv_w, conv_b, a_log_fwd, a_log_bwd, dt_bias_fwd, dt_bias_bwd,
                  d_skip, g_fourier_norm, g_ssd_norm, w_out, g_xattn_norm, g_mem_norm, w_q, w_k,
                  w_v, w_o, g_moe_norm, w_router, b_router, w_mlp1, b_mlp1, w_mlp2, b_mlp2, g_final):
    s1 = FOURIER_W
    s2 = s1 + SSD_W
    s3 = s2 + CONV_CH
    w = w_in[0]
    wdt = jnp.zeros((D_MODEL, 2 * LANES), F32)
    wdt = wdt.at[:, :SSD_HEADS].set(w[:, s3:s3 + SSD_HEADS])
    wdt = wdt.at[:, LANES:LANES + SSD_HEADS].set(w[:, s3 + SSD_HEADS:])
    pad_heads = lambda a, b: jnp.zeros((2, 1, LANES), F32).at[0, 0, :SSD_HEADS].set(a[0]).at[
        1, 0, :SSD_HEADS].set(b[0])
    dsk = jnp.zeros((2, 1, SSD_W), F32).at[0, 0].set(jnp.repeat(d_skip[0], SSD_HEAD_DIM))
    lower = np.tril(np.ones((CHUNK, CHUNK), np.float32))
    return dict(
        g_mix=g_mix_norm[0][None], wu=w[:, :s1].astype(BF16), wz=w[:, s1:s2].astype(BF16),
        wx=w[:, s2:s3].astype(BF16), wdt=wdt.astype(BF16), cs=_channel_dft_table(),
        conv_w=conv_w[0, :, 0, :], conv_b=conv_b[0][None],
        tri=jnp.asarray(np.stack([lower, lower.T])), alog=pad_heads(a_log_fwd, a_log_bwd),
        dtb=pad_heads(dt_bias_fwd, dt_bias_bwd), dsk=dsk,
        g_four=g_fourier_norm[0][None], g_ssd=g_ssd_norm[0][None],
        w4=w_out[0, :FOURIER_W].astype(BF16), wy=w_out[0, FOURIER_W:].astype(BF16),
        g_x=g_xattn_norm[0][None], g_mem=g_mem_norm[0][None],
        wq=w_q[0].astype(BF16), wk=w_k[0].astype(BF16), wv=w_v[0].astype(BF16), wo=w_o[0].astype(BF16),
        g_moe=g_moe_norm[0][None], wr_t=w_router[0].T, br=b_router[0][:, None],
        w1=w_mlp1[0].astype(BF16), b1=b_mlp1[0][:, None, :], w2=w_mlp2[0].astype(BF16),
        b2=b_mlp2[0][:, None, :], g_final=g_final[None])


def _encode_group(x, mem, p):
    bsz, seqlen, _ = x.shape
    t = bsz * seqlen
    tm = min(512, seqlen)
    x2d = x.reshape(t, D_MODEL)

    ab, z, xbc, dt = _inproj(x2d, p["g_mix"], p["wu"], p["wz"], p["wx"], p["wdt"], p["cs"], tm)
    cos_t, nsin_t = _dft_tables(seqlen)
    nb = 4 if bsz % 4 == 0 else (2 if bsz % 2 == 0 else 1)
    tf = min(1024, seqlen)
    four = _fourier(ab.reshape(bsz, seqlen, 2 * FOURIER_W), cos_t, nsin_t, p["g_four"], nb, tf, tf)
    xbc = _conv(xbc.reshape(bsz, seqlen, CONV_CH), p["conv_w"], p["conv_b"], tm, 512)
    y2 = _ssd(xbc, dt.reshape(bsz, seqlen, 2 * LANES), p["tri"], p["alog"], p["dtb"], p["dsk"])
    h = _outproj(x2d, four.reshape(t, FOURIER_W), y2.reshape(2, t, SSD_W), z, p["g_ssd"],
                 p["w4"], p["wy"], tm)

    k, v = _kv(mem, p["g_mem"], p["wk"], p["wv"])
    h = _xattn(h.reshape(bsz, seqlen, D_MODEL), p["g_x"], p["wq"], k, v, p["wo"], tm).reshape(t, D_MODEL)

    td = min(MOE_TILE, t)
    xn, rloc, gate, cnt, car = _router(h, p["g_moe"], p["wr_t"], p["br"], td)
    counts = jnp.sum(cnt, axis=1)
    padded = ((counts + MOE_BLK - 1) // MOE_BLK) * MOE_BLK
    pend = jnp.cumsum(padded)
    pstart = (pend - padded).astype(jnp.int32)
    max_rows = t * TOP_K + (t // td) * N_EXPERTS * (RUN_ALIGN - 1)
    n_blocks = -(-max_rows // MOE_BLK) + N_EXPERTS
    n_slots = n_blocks * MOE_BLK
    n_active = (pend[-1] // MOE_BLK).astype(jnp.int32)
    blk = jnp.arange(n_blocks, dtype=jnp.int32)
    be = jnp.sum(pend[None, :] <= (jnp.minimum(blk, n_active - 1) * MOE_BLK)[:, None], axis=1)
    be = jnp.minimum(be, N_EXPERTS - 1).astype(jnp.int32)
    zstart = (pend - MOE_BLK).astype(jnp.int32)[jnp.argsort(padded == 0)]
    nzero = jnp.stack([jnp.sum(padded > 0).astype(jnp.int32), n_active])

    cnt_flat = cnt.T.reshape(-1)
    car_flat = car.T.reshape(-1)
    xs = _dispatch(pstart, cnt_flat, car_flat, zstart, nzero, xn, rloc, n_slots, td)
    outs = _experts(be, n_active[None], xs, p["w1"], p["b1"], p["w2"], p["b2"])
    y = _combine(pstart, cnt_flat, car_flat, h, rloc, gate, p["g_final"], outs, td)
    return y.reshape(bsz, seqlen, D_MODEL)


def kernel(x_prompt, x_sample, mem_prompt, mem_sample, g_mix_norm, w_in, conv_w, conv_b, a_log_fwd, a_log_bwd, dt_bias_fwd, dt_bias_bwd, d_skip, g_fourier_norm, g_ssd_norm, w_out, g_xattn_norm, g_mem_norm, w_q, w_k, w_v, w_o, g_moe_norm, w_router, b_router, w_mlp1, b_mlp1, w_mlp2, b_mlp2, g_final):
    p = _prep_weights(g_mix_norm, w_in, conv_w, conv_b, a_log_fwd, a_log_bwd, dt_bias_fwd,
                      dt_bias_bwd, d_skip, g_fourier_norm, g_ssd_norm, w_out, g_xattn_norm,
                      g_mem_norm, w_q, w_k, w_v, w_o, g_moe_norm, w_router, b_router, w_mlp1,
                      b_mlp1, w_mlp2, b_mlp2, g_final)
    return (_encode_group(x_prompt, mem_prompt, p), _encode_group(x_sample, mem_sample, p))
```

```python
import functools

import numpy as np
import jax
import jax.numpy as jnp
from jax import lax
from jax.experimental import pallas as pl
from jax.experimental.pallas import tpu as pltpu

F32 = jnp.float32
BF16 = jnp.bfloat16

D_MODEL = 1024
D_MIX = 2 * D_MODEL
FOURIER_W = D_MIX // 4
FOURIER_GROUPS = 8
FOURIER_GROUP_DIM = FOURIER_W // FOURIER_GROUPS
SSD_W = D_MIX - FOURIER_W
SSD_HEAD_DIM = 64
SSD_HEADS = SSD_W // SSD_HEAD_DIM
SSD_GROUPS = 4
SSD_HPG = SSD_HEADS // SSD_GROUPS
D_STATE = 128
CONV_WIDTH = 5
CONV_PAD = CONV_WIDTH // 2
GN = SSD_GROUPS * D_STATE
CONV_CH = SSD_W + 2 * GN
CHUNK = 128
XATTN_HEADS = 4
XATTN_HEAD_DIM = D_MODEL // XATTN_HEADS
N_EXPERTS = 32
TOP_K = 4
D_FF = D_MODEL
SWIGLU_ALPHA = 1.702
SWIGLU_LIMIT = 7.0
NORM_EPS = 1e-5

LANES = 128
HALO = 8
CONV_COLS = 512
TOKEN_TILE = 512
DFT_TILE = 1024
MOE_BLK = 512
MOE_TILE = 256
RUN_ALIGN = 8
VMEM_LIMIT = 48 * 1024 * 1024


def _cparams(sem):
    return pltpu.CompilerParams(dimension_semantics=sem, vmem_limit_bytes=VMEM_LIMIT)


def _rms(x, g):
    return x * lax.rsqrt(jnp.mean(x * x, axis=-1, keepdims=True) + NORM_EPS) * g


def _sigmoid(x):
    return 1.0 / (1.0 + jnp.exp(-x))


def _dot(a, b):
    return jnp.dot(a, b, preferred_element_type=F32)


def _dot_nt(a, b):
    return lax.dot_general(a, b, (((1,), (1,)), ((), ())), preferred_element_type=F32)


def _split3(x):
    h = x.astype(BF16)
    r = x - h.astype(F32)
    m = r.astype(BF16)
    l = (r - m.astype(F32)).astype(BF16)
    return h, m, l


def _inproj_kernel(prev_ref, x_ref, next_ref, g_ref, wu_ref, wz_ref, wx_ref, wdt_ref, cs_ref,
                   cw_ref, cb_ref, ab_ref, z_ref, xbc_ref, dt_ref):
    i = pl.program_id(1)
    tm = x_ref.shape[1]
    rows = tm + 2 * HALO
    mid = slice(HALO, HALO + tm)
    xa = jnp.concatenate([prev_ref[0], x_ref[0], next_ref[0]], axis=0)
    xn = _rms(xa, g_ref[...]).astype(BF16)
    u = _dot(xn, wu_ref[...])[mid]
    ab_ref[...] = _dot(u.astype(BF16), cs_ref[...]).astype(BF16)
    z_ref[...] = _dot(xn, wz_ref[...])[mid].astype(BF16)
    dt_ref[...] = _dot(xn, wdt_ref[...])[mid]

    r = lax.broadcasted_iota(jnp.int32, (rows, 1), 0)
    has_prev = jnp.where(i > 0, 1.0, 0.0)
    has_next = jnp.where(i < pl.num_programs(1) - 1, 1.0, 0.0)
    inside = jnp.where(r < HALO, has_prev, jnp.where(r >= HALO + tm, has_next, 1.0))
    for c in range(CONV_CH // CONV_COLS):
        cols = slice(c * CONV_COLS, (c + 1) * CONV_COLS)
        xe = _dot(xn, wx_ref[:, cols]) * inside
        acc = cb_ref[:, cols] + cw_ref[CONV_PAD:CONV_PAD + 1, cols] * xe
        for w in range(CONV_WIDTH):
            if w != CONV_PAD:
                acc = acc + cw_ref[w:w + 1, cols] * pltpu.roll(xe, (CONV_PAD - w) % rows, axis=0)
        y = acc[mid]
        xbc_ref[:, cols] = (y * _sigmoid(y)).astype(BF16)


def _inproj(x, g, wu, wz, wx, wdt, cs, cw, cb, tm):
    bsz, seqlen, _ = x.shape
    t = bsz * seqlen
    nt = seqlen // tm
    r = tm // HALO
    nh = seqlen // HALO
    full = lambda a: pl.BlockSpec(a.shape, lambda b, i: (0,) * a.ndim)
    row = lambda w: pl.BlockSpec((tm, w), lambda b, i: (b * nt + i, 0))
    return pl.pallas_call(
        _inproj_kernel,
        grid=(bsz, nt),
        in_specs=[pl.BlockSpec((1, HALO, D_MODEL), lambda b, i: (b, jnp.maximum(i * r - 1, 0), 0)),
                  pl.BlockSpec((1, tm, D_MODEL), lambda b, i: (b, i, 0)),
                  pl.BlockSpec((1, HALO, D_MODEL), lambda b, i: (b, jnp.minimum((i + 1) * r, nh - 1), 0)),
                  full(g), full(wu), full(wz), full(wx), full(wdt), full(cs), full(cw), full(cb)],
        out_specs=[row(2 * FOURIER_W), row(SSD_W), row(CONV_CH), row(2 * LANES)],
        out_shape=[jax.ShapeDtypeStruct((t, 2 * FOURIER_W), BF16),
                   jax.ShapeDtypeStruct((t, SSD_W), BF16),
                   jax.ShapeDtypeStruct((t, CONV_CH), BF16),
                   jax.ShapeDtypeStruct((t, 2 * LANES), F32)],
        compiler_params=_cparams(("parallel", "parallel")),
        name="inproj",
    )(x, x, x, g, wu, wz, wx, wdt, cs, cw, cb)


def _fourier_kernel(c_ref, s_ref, ab_ref, g_ref, o_ref, acc_ref, *, scale, nb):
    k = pl.program_id(2)

    @pl.when(k == 0)
    def _():
        acc_ref[...] = jnp.zeros_like(acc_ref)

    c = c_ref[...]
    s = s_ref[...]
    for b in range(nb):
        acc_ref[b] += _dot(c, ab_ref[b, :, :FOURIER_W]) + _dot(s, ab_ref[b, :, FOURIER_W:])

    @pl.when(k == pl.num_programs(2) - 1)
    def _():
        for b in range(nb):
            o_ref[b] = _rms(acc_ref[b] * scale, g_ref[...]).astype(BF16)


def _fourier(ab, cos_t, nsin_t, g, nb, tr, tk):
    bsz, seqlen, _ = ab.shape
    scale = float(1.0 / np.sqrt(seqlen * FOURIER_GROUP_DIM))
    return pl.pallas_call(
        functools.partial(_fourier_kernel, scale=scale, nb=nb),
        grid=(bsz // nb, seqlen // tr, seqlen // tk),
        in_specs=[pl.BlockSpec((tr, tk), lambda b, i, k: (i, k)),
                  pl.BlockSpec((tr, tk), lambda b, i, k: (i, k)),
                  pl.BlockSpec((nb, tk, 2 * FOURIER_W), lambda b, i, k: (b, k, 0)),
                  pl.BlockSpec((1, FOURIER_W), lambda b, i, k: (0, 0))],
        out_specs=pl.BlockSpec((nb, tr, FOURIER_W), lambda b, i, k: (b, i, 0)),
        out_shape=jax.ShapeDtypeStruct((bsz, seqlen, FOURIER_W), BF16),
        scratch_shapes=[pltpu.VMEM((nb, tr, FOURIER_W), F32)],
        compiler_params=_cparams(("parallel", "parallel", "arbitrary")),
        name="fourier",
    )(cos_t, nsin_t, ab, g)


def _dft_tables(seqlen):
    lo = min(seqlen, LANES)
    hi = seqlen // lo
    k = np.arange(seqlen, dtype=np.int64)
    a_hi = 2.0 * np.pi * ((np.arange(hi, dtype=np.int64)[:, None] * lo * k[None, :]) % seqlen) / seqlen
    a_lo = 2.0 * np.pi * ((np.arange(lo, dtype=np.int64)[:, None] * k[None, :]) % seqlen) / seqlen
    c1 = jnp.asarray(np.cos(a_hi), F32)[:, None, :]
    s1 = jnp.asarray(np.sin(a_hi), F32)[:, None, :]
    c0 = jnp.asarray(np.cos(a_lo), F32)[None, :, :]
    s0 = jnp.asarray(np.sin(a_lo), F32)[None, :, :]
    cos_t = (c1 * c0 - s1 * s0).reshape(seqlen, seqlen).astype(BF16)
    nsin_t = (-(s1 * c0 + c1 * s0)).reshape(seqlen, seqlen).astype(BF16)
    return cos_t, nsin_t


def _channel_dft_table():
    c = np.arange(FOURIER_W)
    same = (c[:, None] // FOURIER_GROUP_DIM) == (c[None, :] // FOURIER_GROUP_DIM)
    ang = 2.0 * np.pi * ((c[:, None] % FOURIER_GROUP_DIM) * (c[None, :] % FOURIER_GROUP_DIM)
                         % FOURIER_GROUP_DIM) / FOURIER_GROUP_DIM
    cc = np.where(same, np.cos(ang), 0.0)
    sc = np.where(same, np.sin(ang), 0.0)
    return jnp.asarray(np.concatenate([cc, sc], axis=1), BF16)


def _ssd_kernel(xbc_ref, dt_ref, tri_ref, alog_ref, dtb_ref, dsk_ref, y_ref, state_ref):
    q = CHUNK

    @pl.when(pl.program_id(2) == 0)
    def _():
        state_ref[...] = jnp.zeros_like(state_ref)

    tri = tri_ref[0]
    mask = tri > 0.5
    a = -jnp.exp(alog_ref[0])
    dtr = dt_ref[0] + dtb_ref[0]
    dt = jnp.maximum(dtr, 0.0) + jnp.log1p(jnp.exp(-jnp.abs(dtr)))
    da = dt * a
    tri_b = tri.astype(BF16)
    d0, d1, d2 = _split3(da)
    acum = _dot(tri_b, d0) + _dot(tri_b, d1) + _dot(tri_b, d2)
    total = jnp.sum(da, axis=0, keepdims=True)
    eacum = jnp.exp(acum)
    etot = jnp.exp(total)
    acum_t = acum.T
    dt_t = dt.T
    w_t = (dt * jnp.exp(total - acum)).T
    lane_lo = lax.broadcasted_iota(jnp.int32, (1, LANES), 1) < SSD_HEAD_DIM
    zero_b = jnp.zeros((), BF16)

    for g in range(SSD_GROUPS):
        bm = xbc_ref[0, :, SSD_W + g * D_STATE:SSD_W + (g + 1) * D_STATE]
        cm = xbc_ref[0, :, SSD_W + GN + g * D_STATE:SSD_W + GN + (g + 1) * D_STATE]
        cb = _dot_nt(cm, bm)
        cm_f = cm.astype(F32)
        bm_t = bm.astype(F32).T
        for p in range(SSD_HPG // 2):
            t = g * (SSD_HPG // 2) + p
            cols = slice(t * LANES, (t + 1) * LANES)
            xs = xbc_ref[0, :, cols]
            st = state_ref[:, cols]
            xs_lo = jnp.where(lane_lo, xs, zero_b)
            xs_hi = jnp.where(lane_lo, zero_b, xs)
            st_b = st.astype(BF16)
            st_lo = jnp.where(lane_lo, st_b, zero_b)
            st_hi = jnp.where(lane_lo, zero_b, st_b)
            lhs = []
            lhs2 = []
            for h in (2 * t, 2 * t + 1):
                colb = jnp.broadcast_to(acum[:, h:h + 1], (q, q))
                decay = jnp.where(mask, jnp.exp(colb - acum_t[h:h + 1, :]), 0.0)
                lhs.append((cb * decay * dt_t[h:h + 1, :]).astype(BF16))
                lhs.append((cm_f * jnp.broadcast_to(eacum[:, h:h + 1], (q, D_STATE))).astype(BF16))
                lhs2.append((bm_t * w_t[h:h + 1, :]).astype(BF16))
            y = _dot(jnp.concatenate(lhs, axis=1),
                     jnp.concatenate([xs_lo, st_lo, xs_hi, st_hi], axis=0))
            y = y + dsk_ref[0, :, cols] * xs.astype(F32)
            y_ref[0, 0, :, cols] = y.astype(BF16)
            dec = jnp.where(lane_lo, etot[:, 2 * t:2 * t + 1], etot[:, 2 * t + 1:2 * t + 2])
            state_ref[:, cols] = st * dec + _dot(jnp.concatenate(lhs2, axis=1),
                                                 jnp.concatenate([xs_lo, xs_hi], axis=0))


def _ssd(xbc, dt, tri, alog, dtb, dsk):
    bsz, seqlen, _ = xbc.shape
    nc = seqlen // CHUNK
    chunk = lambda d, j: j + d * (nc - 1 - 2 * j)
    return pl.pallas_call(
        _ssd_kernel,
        grid=(bsz, 2, nc),
        in_specs=[pl.BlockSpec((1, CHUNK, CONV_CH), lambda b, d, j: (b, chunk(d, j), 0)),
                  pl.BlockSpec((1, CHUNK, LANES), lambda b, d, j: (b, chunk(d, j), d)),
                  pl.BlockSpec((1, CHUNK, CHUNK), lambda b, d, j: (d, 0, 0)),
                  pl.BlockSpec((1, 1, LANES), lambda b, d, j: (d, 0, 0)),
                  pl.BlockSpec((1, 1, LANES), lambda b, d, j: (d, 0, 0)),
                  pl.BlockSpec((1, 1, SSD_W), lambda b, d, j: (d, 0, 0))],
        out_specs=pl.BlockSpec((1, 1, CHUNK, SSD_W), lambda b, d, j: (d, b, chunk(d, j), 0)),
        out_shape=jax.ShapeDtypeStruct((2, bsz, seqlen, SSD_W), BF16),
        scratch_shapes=[pltpu.VMEM((D_STATE, SSD_W), F32)],
        compiler_params=_cparams(("parallel", "arbitrary", "arbitrary")),
        name="ssd",
    )(xbc, dt, tri, alog, dtb, dsk)


def _kv_kernel(mem_ref, g_ref, wk_ref, wv_ref, k_ref, v_ref):
    mn = _rms(mem_ref[0], g_ref[...]).astype(BF16)
    k_ref[0] = _dot(mn, wk_ref[...]).astype(BF16)
    v_ref[0] = _dot(mn, wv_ref[...]).astype(BF16)


def _kv(mem, g, wk, wv):
    bsz, nm, _ = mem.shape
    full = lambda a: pl.BlockSpec(a.shape, lambda b: (0,) * a.ndim)
    blk = pl.BlockSpec((1, nm, D_MODEL), lambda b: (b, 0, 0))
    return pl.pallas_call(
        _kv_kernel,
        grid=(bsz,),
        in_specs=[blk, full(g), full(wk), full(wv)],
        out_specs=[blk, blk],
        out_shape=[jax.ShapeDtypeStruct(mem.shape, BF16)] * 2,
        compiler_params=_cparams(("parallel",)),
        name="kv",
    )(mem, g, wk, wv)


def _cross_attend(h, g, wq_ref, k_ref, v_ref, wo_ref):
    hn = _rms(h, g).astype(BF16)
    q = (_dot(hn, wq_ref[...]) * (XATTN_HEAD_DIM ** -0.5)).astype(BF16)
    outs = []
    for hd in range(XATTN_HEADS):
        cols = slice(hd * XATTN_HEAD_DIM, (hd + 1) * XATTN_HEAD_DIM)
        s = _dot_nt(q[:, cols], k_ref[0, :, cols])
        e = jnp.exp(s - jnp.max(s, axis=-1, keepdims=True))
        p = (e / jnp.sum(e, axis=-1, keepdims=True)).astype(BF16)
        outs.append(_dot(p, v_ref[0, :, cols]).astype(BF16))
    return h + _dot(jnp.concatenate(outs, axis=1), wo_ref[...])


def _route_tile(xn, i, cols, w0, w1, br, rloc_ref, gate_ref, carry_ref, cnt_acc, car_acc):
    tm = xn.shape[0]
    x0, x1, _ = _split3(xn)
    logits = _dot_nt(w0, x0) + _dot_nt(w0, x1) + _dot_nt(w1, x0) + br

    eid = lax.broadcasted_iota(jnp.int32, (N_EXPERTS, tm), 0)
    work = logits
    vals, hits = [], []
    sel = jnp.zeros((N_EXPERTS, tm), F32)
    for k in range(TOP_K):
        m = jnp.max(work, axis=0, keepdims=True)
        ik = jnp.min(jnp.where(work == m, eid, N_EXPERTS), axis=0, keepdims=True)
        hit = eid == ik
        vals.append(m)
        hits.append(hit)
        sel = jnp.where(hit, 1.0, sel)
        work = jnp.where(hit, -jnp.inf, work)
    es = [jnp.exp(v - vals[0]) for v in vals]
    den = es[0] + es[1] + es[2] + es[3]
    for k in range(TOP_K):
        gate_ref[k:k + 1, cols] = es[k] / den

    r = lax.broadcasted_iota(jnp.int32, (tm, tm), 0)
    c = lax.broadcasted_iota(jnp.int32, (tm, tm), 1)
    before = jnp.where(r < c, 1.0, 0.0).astype(BF16)
    cum = _dot(sel.astype(BF16), before)
    counts = jnp.sum(sel, axis=1, keepdims=True)
    counts = jnp.floor((counts + (RUN_ALIGN - 1)) * (1.0 / RUN_ALIGN)) * RUN_ALIGN
    er = lax.broadcasted_iota(jnp.int32, (N_EXPERTS, N_EXPERTS), 0)
    ec = lax.broadcasted_iota(jnp.int32, (N_EXPERTS, N_EXPERTS), 1)
    lower = jnp.where(er > ec, 1.0, 0.0).astype(BF16)
    offs = _dot(lower, jnp.broadcast_to(counts, (N_EXPERTS, LANES)).astype(BF16))[:, 0:1]
    pos = cum + offs
    for k in range(TOP_K):
        rk = jnp.sum(jnp.where(hits[k], pos, 0.0), axis=0, keepdims=True)
        rloc_ref[k:k + 1, cols] = rk.astype(jnp.int32)

    lane = lax.broadcasted_iota(jnp.int32, cnt_acc.shape, 1)
    cnt_acc[...] = jnp.where(lane == i, counts, cnt_acc[...])
    car_acc[...] = jnp.where(lane == i, carry_ref[:, 0:1], car_acc[...])
    carry_ref[...] = carry_ref[...] + counts


def _post_kernel(x_ref, four_ref, y_ref, z_ref, gs_ref, w4_ref, wy_ref, gx_ref, wq_ref, k_ref, v_ref,
                 wo_ref, gm_ref, wr_ref, br_ref, h_ref, xn_ref, rloc_ref, gate_ref, cnt_ref, car_ref,
                 carry_ref, cnt_acc, car_acc):
    step = pl.program_id(0) * pl.num_programs(1) + pl.program_id(1)
    tm = x_ref.shape[1]

    @pl.when(step == 0)
    def _():
        carry_ref[...] = jnp.zeros_like(carry_ref)
        cnt_acc[...] = jnp.zeros_like(cnt_acc)
        car_acc[...] = jnp.zeros_like(car_acc)

    y = y_ref[0, 0].astype(F32) + y_ref[1, 0].astype(F32)
    z = z_ref[0].astype(F32)
    yn = _rms(y * (z * _sigmoid(z)), gs_ref[...]).astype(BF16)
    h = x_ref[0] + _dot(four_ref[0], w4_ref[...]) + _dot(yn, wy_ref[...])

    h = _cross_attend(h, gx_ref[...], wq_ref, k_ref, v_ref, wo_ref)
    h_ref[0] = h

    xn = _rms(h, gm_ref[...])
    xn_ref[0] = xn.astype(BF16)
    w0, w1, _ = _split3(wr_ref[...])
    sub = tm // MOE_TILE
    for s in range(sub):
        rows = slice(s * MOE_TILE, (s + 1) * MOE_TILE)
        _route_tile(xn[rows], step * sub + s, rows, w0, w1, br_ref[...], rloc_ref, gate_ref,
                    carry_ref, cnt_acc, car_acc)
    cnt_ref[...] = cnt_acc[...].astype(jnp.int32)
    car_ref[...] = car_acc[...].astype(jnp.int32)


def _post(x, four, y2, z, mem_k, mem_v, p, tm):
    bsz, seqlen, _ = x.shape
    t = bsz * seqlen
    nt = seqlen // tm
    n_tiles = t // MOE_TILE
    nm = mem_k.shape[1]
    weights = [p["g_ssd"], p["w4"], p["wy"], p["g_x"], p["wq"]]
    weights2 = [p["wo"], p["g_moe"], p["wr_t"], p["br"]]
    full = lambda a: pl.BlockSpec(a.shape, lambda b, i: (0,) * a.ndim)
    tok = lambda w: pl.BlockSpec((1, tm, w), lambda b, i: (b, i, 0))
    mem = pl.BlockSpec((1, nm, D_MODEL), lambda b, i: (b, 0, 0))
    kt = pl.BlockSpec((TOP_K, tm), lambda b, i: (0, b * nt + i))
    per_tile = pl.BlockSpec((N_EXPERTS, n_tiles), lambda b, i: (0, 0))
    return pl.pallas_call(
        _post_kernel,
        grid=(bsz, nt),
        in_specs=[tok(D_MODEL), tok(FOURIER_W),
                  pl.BlockSpec((2, 1, tm, SSD_W), lambda b, i: (0, b, i, 0)), tok(SSD_W)]
                 + [full(a) for a in weights] + [mem, mem] + [full(a) for a in weights2],
        out_specs=[tok(D_MODEL), tok(D_MODEL), kt, kt, per_tile, per_tile],
        out_shape=[jax.ShapeDtypeStruct((bsz, seqlen, D_MODEL), F32),
                   jax.ShapeDtypeStruct((bsz, seqlen, D_MODEL), BF16),
                   jax.ShapeDtypeStruct((TOP_K, t), jnp.int32),
                   jax.ShapeDtypeStruct((TOP_K, t), F32),
                   jax.ShapeDtypeStruct((N_EXPERTS, n_tiles), jnp.int32),
                   jax.ShapeDtypeStruct((N_EXPERTS, n_tiles), jnp.int32)],
        scratch_shapes=[pltpu.VMEM((N_EXPERTS, LANES), F32),
                        pltpu.VMEM((N_EXPERTS, n_tiles), F32),
                        pltpu.VMEM((N_EXPERTS, n_tiles), F32)],
        compiler_params=_cparams(("arbitrary", "arbitrary")),
        name="post",
    )(x, four, y2, z, *weights, mem_k, mem_v, *weights2)


def _pow2_pieces(n, max_size, piece_fn):
    done = jnp.int32(0)
    size = max_size
    while size >= RUN_ALIGN:
        bit = n & size

        @pl.when(bit != 0)
        def _(size=size, done=done):
            piece_fn(done, size)

        done = done + bit
        size //= 2


def _for_each_run(tile, cnt_ref, car_ref, pstart_ref, tile_rows, piece_fn):
    def per_expert(e, local):
        n = cnt_ref[tile * N_EXPERTS + e]
        local = pl.multiple_of(local, RUN_ALIGN)
        slot = pl.multiple_of(pstart_ref[e] + car_ref[tile * N_EXPERTS + e], RUN_ALIGN)
        _pow2_pieces(n, tile_rows, lambda done, size: piece_fn(
            pl.multiple_of(local + done, RUN_ALIGN), pl.multiple_of(slot + done, RUN_ALIGN), size))
        return local + n

    return lax.fori_loop(0, N_EXPERTS, per_expert, jnp.int32(0))


def _sorted_rows(tile_rows):
    return TOP_K * tile_rows + N_EXPERTS * RUN_ALIGN


def _sorted_onehot(rloc_ref, tile_rows):
    rows = lax.broadcasted_iota(jnp.int32, (_sorted_rows(tile_rows), tile_rows), 0)
    return [rows == rloc_ref[k:k + 1, :] for k in range(TOP_K)]


def _dispatch_kernel(pstart_ref, cnt_ref, car_ref, zstart_ref, nzero_ref, x_ref, rloc_ref, xs_hbm,
                     ybuf, zbuf, total_s, sem_z, sem_y):
    i = pl.program_id(0)
    td = x_ref.shape[0]
    slot = i % 2
    half = D_MODEL // 2

    def wait_slot(s):
        def wait_piece(done, size):
            del done
            pltpu.make_async_copy(ybuf.at[s, pl.ds(0, size)], xs_hbm.at[pl.ds(0, size)],
                                  sem_y.at[s]).wait()

        _pow2_pieces(total_s[s], TOP_K * td, wait_piece)

    @pl.when(i == 0)
    def _():
        zbuf[...] = jnp.zeros_like(zbuf)

        def zcopy(start):
            dst = pl.ds(pl.multiple_of(start, MOE_BLK), MOE_BLK)
            return pltpu.make_async_copy(zbuf, xs_hbm.at[dst], sem_z)

        def issue(j, carry):
            zcopy(zstart_ref[j]).start()
            return carry

        def issue_tail(j, carry):
            zcopy(j * MOE_BLK).start()
            return carry

        def drain(j, carry):
            zcopy(0).wait()
            return carry

        n_blocks = xs_hbm.shape[0] // MOE_BLK
        lax.fori_loop(0, nzero_ref[0], issue, 0)
        lax.fori_loop(nzero_ref[1], n_blocks, issue_tail, 0)
        lax.fori_loop(0, nzero_ref[0] + (n_blocks - nzero_ref[1]), drain, 0)

    @pl.when(i >= 2)
    def _():
        wait_slot(slot)

    onehot = jnp.zeros((_sorted_rows(td), td), F32)
    for hit in _sorted_onehot(rloc_ref, td):
        onehot = jnp.where(hit, 1.0, onehot)
    y = lax.bitcast_convert_type(_dot(onehot.astype(BF16), x_ref[...]), jnp.uint32)
    ybuf[slot] = (y[:, :half] >> 16) | (y[:, half:] & jnp.uint32(0xFFFF0000))

    def send(local, dst, size):
        pltpu.make_async_copy(ybuf.at[slot, pl.ds(local, size)], xs_hbm.at[pl.ds(dst, size)],
                              sem_y.at[slot]).start()

    total_s[slot] = _for_each_run(i, cnt_ref, car_ref, pstart_ref, td, send)

    @pl.when(i == pl.num_programs(0) - 1)
    def _():
        wait_slot(slot)

        @pl.when(i >= 1)
        def _():
            wait_slot(1 - slot)


def _dispatch(pstart, cnt, car, zstart, nzero, xn, rloc, n_slots, td):
    t = xn.shape[0]
    half = D_MODEL // 2
    return pl.pallas_call(
        _dispatch_kernel,
        grid_spec=pltpu.PrefetchScalarGridSpec(
            num_scalar_prefetch=5,
            grid=(t // td,),
            in_specs=[pl.BlockSpec((td, D_MODEL), lambda i, *_: (i, 0)),
                      pl.BlockSpec((TOP_K, td), lambda i, *_: (0, i))],
            out_specs=pl.BlockSpec(memory_space=pl.ANY),
            scratch_shapes=[pltpu.VMEM((2, _sorted_rows(td), half), jnp.uint32),
                            pltpu.VMEM((MOE_BLK, half), jnp.uint32),
                            pltpu.SMEM((2,), jnp.int32),
                            pltpu.SemaphoreType.DMA,
                            pltpu.SemaphoreType.DMA((2,))]),
        out_shape=jax.ShapeDtypeStruct((n_slots, half), jnp.uint32),
        compiler_params=_cparams(("arbitrary",)),
        name="dispatch",
    )(pstart, cnt, car, zstart, nzero, xn, rloc)


def _expert_kernel(be_ref, na_ref, xs_ref, w1_ref, b1_ref, w2_ref, b2_ref, o_ref):
    i = pl.program_id(0)

    @pl.when(i < na_ref[0])
    def _():
        w = xs_ref[...]
        lo = lax.bitcast_convert_type(w << 16, F32)
        hi = lax.bitcast_convert_type(w & jnp.uint32(0xFFFF0000), F32)
        xb = jnp.concatenate([lo, hi], axis=1).astype(BF16)
        h = _dot(xb, w1_ref[0]) + b1_ref[0]
        glu = jnp.minimum(h[:, :D_FF], SWIGLU_LIMIT)
        lin = jnp.clip(h[:, D_FF:], -SWIGLU_LIMIT, SWIGLU_LIMIT)
        act = glu * _sigmoid(SWIGLU_ALPHA * glu) * (lin + 1.0)
        o_ref[...] = _dot(act.astype(BF16), w2_ref[0]) + b2_ref[0]

    @pl.when(i >= na_ref[0])
    def _():
        o_ref[...] = jnp.zeros_like(o_ref)


def _experts(block_e, n_active, xs, w1, b1, w2, b2):
    n_slots = xs.shape[0]
    return pl.pallas_call(
        _expert_kernel,
        grid_spec=pltpu.PrefetchScalarGridSpec(
            num_scalar_prefetch=2,
            grid=(n_slots // MOE_BLK,),
            in_specs=[pl.BlockSpec((MOE_BLK, D_MODEL // 2),
                                   lambda i, be, na: (jnp.minimum(i, na[0] - 1), 0)),
                      pl.BlockSpec((1, D_MODEL, 2 * D_FF), lambda i, be, na: (be[i], 0, 0)),
                      pl.BlockSpec((1, 1, 2 * D_FF), lambda i, be, na: (be[i], 0, 0)),
                      pl.BlockSpec((1, D_FF, D_MODEL), lambda i, be, na: (be[i], 0, 0)),
                      pl.BlockSpec((1, 1, D_MODEL), lambda i, be, na: (be[i], 0, 0))],
            out_specs=pl.BlockSpec((MOE_BLK, D_MODEL), lambda i, be, na: (i, 0))),
        out_shape=jax.ShapeDtypeStruct((n_slots, D_MODEL), F32),
        compiler_params=_cparams(("arbitrary",)),
        name="experts",
    )(block_e, n_active, xs, w1, b1, w2, b2)


def _combine_kernel(pstart_ref, cnt_ref, car_ref, h_ref, rloc_ref, gate_ref, gf_ref, outs_hbm, o_ref,
                    ybuf, total_s, sem_y):
    i = pl.program_id(0)
    nt = pl.num_programs(0)
    tc = h_ref.shape[0]
    slot = i % 2

    def fetch(tile, s):
        def piece(local, src, size):
            pltpu.make_async_copy(outs_hbm.at[pl.ds(src, size)], ybuf.at[s, pl.ds(local, size)],
                                  sem_y.at[s]).start()

        total_s[s] = _for_each_run(tile, cnt_ref, car_ref, pstart_ref, tc, piece)

    @pl.when(i == 0)
    def _():
        fetch(i, slot)

    @pl.when(i + 1 < nt)
    def _():
        fetch(i + 1, 1 - slot)

    hits = _sorted_onehot(rloc_ref, tc)
    gsel = jnp.zeros((_sorted_rows(tc), tc), F32)
    for k in range(TOP_K):
        gsel = jnp.where(hits[k], gate_ref[k:k + 1, :], gsel)
    g_t = gsel.T
    g0 = g_t.astype(BF16)
    g1 = (g_t - g0.astype(F32)).astype(BF16)

    def wait_piece(done, size):
        del done
        pltpu.make_async_copy(outs_hbm.at[pl.ds(0, size)], ybuf.at[slot, pl.ds(0, size)],
                              sem_y.at[slot]).wait()

    _pow2_pieces(total_s[slot], TOP_K * tc, wait_piece)
    row = lax.broadcasted_iota(jnp.int32, (_sorted_rows(tc), 1), 0)
    yb = jnp.where(row < total_s[slot], ybuf[slot], 0.0).astype(BF16)
    o_ref[...] = _rms(h_ref[...] + _dot(g0, yb) + _dot(g1, yb), gf_ref[...])


def _combine(pstart, cnt, car, h2d, rloc, gate, gf, outs, tc):
    t = h2d.shape[0]
    kt = pl.BlockSpec((TOP_K, tc), lambda i, *_: (0, i))
    return pl.pallas_call(
        _combine_kernel,
        grid_spec=pltpu.PrefetchScalarGridSpec(
            num_scalar_prefetch=3,
            grid=(t // tc,),
            in_specs=[pl.BlockSpec((tc, D_MODEL), lambda i, *_: (i, 0)), kt, kt,
                      pl.BlockSpec((1, D_MODEL), lambda i, *_: (0, 0)),
                      pl.BlockSpec(memory_space=pl.ANY)],
            out_specs=pl.BlockSpec((tc, D_MODEL), lambda i, *_: (i, 0)),
            scratch_shapes=[pltpu.VMEM((2, _sorted_rows(tc), D_MODEL), F32),
                            pltpu.SMEM((2,), jnp.int32),
                            pltpu.SemaphoreType.DMA((2,))]),
        out_shape=jax.ShapeDtypeStruct((t, D_MODEL), F32),
        compiler_params=_cparams(("arbitrary",)),
        name="combine",
    )(pstart, cnt, car, h2d, rloc, gate, gf, outs)


def _prep_weights(g_mix_norm, w_in, conv_w, conv_b, a_log_fwd, a_log_bwd, dt_bias_fwd, dt_bias_bwd,
                  d_skip, g_fourier_norm, g_ssd_norm, w_out, g_xattn_norm, g_mem_norm, w_q, w_k,
                  w_v, w_o, g_moe_norm, w_router, b_router, w_mlp1, b_mlp1, w_mlp2, b_mlp2, g_final):
    s1 = FOURIER_W
    s2 = s1 + SSD_W
    s3 = s2 + CONV_CH
    w = w_in[0]
    wdt = jnp.zeros((D_MODEL, 2 * LANES), F32)
    wdt = wdt.at[:, :SSD_HEADS].set(w[:, s3:s3 + SSD_HEADS])
    wdt = wdt.at[:, LANES:LANES + SSD_HEADS].set(w[:, s3 + SSD_HEADS:])
    pad_heads = lambda a, b: jnp.zeros((2, 1, LANES), F32).at[0, 0, :SSD_HEADS].set(a[0]).at[
        1, 0, :SSD_HEADS].set(b[0])
    dsk = jnp.zeros((2, 1, SSD_W), F32).at[0, 0].set(jnp.repeat(d_skip[0], SSD_HEAD_DIM))
    lower = np.tril(np.ones((CHUNK, CHUNK), np.float32))
    return dict(
        g_mix=g_mix_norm[0][None], wu=w[:, :s1].astype(BF16), wz=w[:, s1:s2].astype(BF16),
        wx=w[:, s2:s3].astype(BF16), wdt=wdt.astype(BF16), cs=_channel_dft_table(),
        conv_w=conv_w[0, :, 0, :], conv_b=conv_b[0][None],
        tri=jnp.asarray(np.stack([lower, lower.T])), alog=pad_heads(a_log_fwd, a_log_bwd),
        dtb=pad_heads(dt_bias_fwd, dt_bias_bwd), dsk=dsk,
        g_four=g_fourier_norm[0][None], g_ssd=g_ssd_norm[0][None],
        w4=w_out[0, :FOURIER_W].astype(BF16), wy=w_out[0, FOURIER_W:].astype(BF16),
        g_x=g_xattn_norm[0][None], g_mem=g_mem_norm[0][None],
        wq=w_q[0].astype(BF16), wk=w_k[0].astype(BF16), wv=w_v[0].astype(BF16), wo=w_o[0].astype(BF16),
        g_moe=g_moe_norm[0][None], wr_t=w_router[0].T, br=b_router[0][:, None],
        w1=w_mlp1[0].astype(BF16), b1=b_mlp1[0][:, None, :], w2=w_mlp2[0].astype(BF16),
        b2=b_mlp2[0][:, None, :], g_final=g_final[None])


def _encode_group(x, mem, p):
    bsz, seqlen, _ = x.shape
    t = bsz * seqlen
    tm = min(TOKEN_TILE, seqlen)

    ab, z, xbc, dt = _inproj(x, p["g_mix"], p["wu"], p["wz"], p["wx"], p["wdt"], p["cs"],
                             p["conv_w"], p["conv_b"], tm)
    cos_t, nsin_t = _dft_tables(seqlen)
    nb = 4 if bsz % 4 == 0 else (2 if bsz % 2 == 0 else 1)
    tf = min(DFT_TILE, seqlen)
    four = _fourier(ab.reshape(bsz, seqlen, 2 * FOURIER_W), cos_t, nsin_t, p["g_four"], nb, tf, tf)
    y2 = _ssd(xbc.reshape(bsz, seqlen, CONV_CH), dt.reshape(bsz, seqlen, 2 * LANES), p["tri"],
              p["alog"], p["dtb"], p["dsk"])

    k, v = _kv(mem, p["g_mem"], p["wk"], p["wv"])
    h, xn, rloc, gate, cnt, car = _post(x, four, y2, z.reshape(bsz, seqlen, SSD_W), k, v, p, tm)
    h = h.reshape(t, D_MODEL)
    xn = xn.reshape(t, D_MODEL)

    td = MOE_TILE
    counts = jnp.sum(cnt, axis=1)
    padded = ((counts + MOE_BLK - 1) // MOE_BLK) * MOE_BLK
    pend = jnp.cumsum(padded)
    pstart = (pend - padded).astype(jnp.int32)
    max_rows = t * TOP_K + (t // td) * N_EXPERTS * (RUN_ALIGN - 1)
    n_blocks = -(-max_rows // MOE_BLK) + N_EXPERTS
    n_slots = n_blocks * MOE_BLK
    n_active = (pend[-1] // MOE_BLK).astype(jnp.int32)
    blk = jnp.arange(n_blocks, dtype=jnp.int32)
    be = jnp.sum(pend[None, :] <= (jnp.minimum(blk, n_active - 1) * MOE_BLK)[:, None], axis=1)
    be = jnp.minimum(be, N_EXPERTS - 1).astype(jnp.int32)
    zstart = (pend - MOE_BLK).astype(jnp.int32)[jnp.argsort(padded == 0)]
    nzero = jnp.stack([jnp.sum(padded > 0).astype(jnp.int32), n_active])

    cnt_flat = cnt.T.reshape(-1)
    car_flat = car.T.reshape(-1)
    xs = _dispatch(pstart, cnt_flat, car_flat, zstart, nzero, xn, rloc, n_slots, td)
    outs = _experts(be, n_active[None], xs, p["w1"], p["b1"], p["w2"], p["b2"])
    y = _combine(pstart, cnt_flat, car_flat, h, rloc, gate, p["g_final"], outs, td)
    return y.reshape(bsz, seqlen, D_MODEL)


def kernel(x_prompt, x_sample, mem_prompt, mem_sample, g_mix_norm, w_in, conv_w, conv_b, a_log_fwd, a_log_bwd, dt_bias_fwd, dt_bias_bwd, d_skip, g_fourier_norm, g_ssd_norm, w_out, g_xattn_norm, g_mem_norm, w_q, w_k, w_v, w_o, g_moe_norm, w_router, b_router, w_mlp1, b_mlp1, w_mlp2, b_mlp2, g_final):
    p = _prep_weights(g_mix_norm, w_in, conv_w, conv_b, a_log_fwd, a_log_bwd, dt_bias_fwd,
                      dt_bias_bwd, d_skip, g_fourier_norm, g_ssd_norm, w_out, g_xattn_norm,
                      g_mem_norm, w_q, w_k, w_v, w_o, g_moe_norm, w_router, b_router, w_mlp1,
                      b_mlp1, w_mlp2, b_mlp2, g_final)
    return (_encode_group(x_prompt, mem_prompt, p), _encode_group(x_sample, mem_sample, p))
```

```python
import functools

import numpy as np
import jax
import jax.numpy as jnp
from jax import lax
from jax.experimental import pallas as pl
from jax.experimental.pallas import tpu as pltpu

F32 = jnp.float32
BF16 = jnp.bfloat16

D_MODEL = 1024
D_MIX = 2 * D_MODEL
FOURIER_W = D_MIX // 4
FOURIER_GROUPS = 8
FOURIER_GROUP_DIM = FOURIER_W // FOURIER_GROUPS
SSD_W = D_MIX - FOURIER_W
SSD_HEAD_DIM = 64
SSD_HEADS = SSD_W // SSD_HEAD_DIM
SSD_GROUPS = 4
SSD_HPG = SSD_HEADS // SSD_GROUPS
D_STATE = 128
CONV_WIDTH = 5
CONV_PAD = CONV_WIDTH // 2
GN = SSD_GROUPS * D_STATE
CONV_CH = SSD_W + 2 * GN
CHUNK = 128
XATTN_HEADS = 4
XATTN_HEAD_DIM = D_MODEL // XATTN_HEADS
N_EXPERTS = 32
TOP_K = 4
D_FF = D_MODEL
SWIGLU_ALPHA = 1.702
SWIGLU_LIMIT = 7.0
NORM_EPS = 1e-5
LOG2E = 1.4426950408889634

LANES = 128
HALO = 8
CONV_COLS = 512
TOKEN_TILE = 512
DFT_TILE = 1024
MOE_BLK = 512
MOE_TILE = 256
RUN_ALIGN = 8
VMEM_LIMIT = 48 * 1024 * 1024


def _cparams(sem):
    return pltpu.CompilerParams(dimension_semantics=sem, vmem_limit_bytes=VMEM_LIMIT)


def _rms(x, g):
    return x * lax.rsqrt(jnp.mean(x * x, axis=-1, keepdims=True) + NORM_EPS) * g


def _sigmoid(x):
    return 1.0 / (1.0 + jnp.exp(-x))


def _softplus(x):
    return jnp.maximum(x, 0.0) + jnp.log1p(jnp.exp(-jnp.abs(x)))


def _dot(a, b):
    return jnp.dot(a, b, preferred_element_type=F32)


def _dot_nt(a, b):
    return lax.dot_general(a, b, (((1,), (1,)), ((), ())), preferred_element_type=F32)


def _pack_halves(x):
    c = x.shape[1] // 2
    u = lax.bitcast_convert_type(x, jnp.uint32)
    return (u[:, :c] >> 16) | (u[:, c:] & jnp.uint32(0xFFFF0000))


def _unpack_halves(w):
    lo = lax.bitcast_convert_type(w << 16, F32)
    hi = lax.bitcast_convert_type(w & jnp.uint32(0xFFFF0000), F32)
    return jnp.concatenate([lo, hi], axis=1).astype(BF16)


def _split3(x):
    h = x.astype(BF16)
    r = x - h.astype(F32)
    m = r.astype(BF16)
    l = (r - m.astype(F32)).astype(BF16)
    return h, m, l


def _inproj_kernel(prev_ref, x_ref, next_ref, g_ref, wu_ref, wz_ref, wx_ref, wdt_ref, wdtt_ref,
                   cs_ref, cw_ref, cb_ref, ab_ref, z_ref, xbc_ref, dt_ref, dtt_ref):
    i = pl.program_id(1)
    tm = x_ref.shape[1]
    rows = tm + 2 * HALO
    mid = slice(HALO, HALO + tm)
    xa = jnp.concatenate([prev_ref[0], x_ref[0], next_ref[0]], axis=0)
    xn = _rms(xa, g_ref[...]).astype(BF16)
    u = _dot(xn, wu_ref[...])[mid]
    ab_ref[...] = _dot(u.astype(BF16), cs_ref[...]).astype(BF16)
    z_ref[...] = _dot(xn, wz_ref[...])[mid].astype(BF16)
    dt_ref[...] = _dot(xn, wdt_ref[...])[mid]
    dtt_ref[...] = _dot_nt(wdtt_ref[...], xn)[:, mid]

    r = lax.broadcasted_iota(jnp.int32, (rows, 1), 0)
    has_prev = jnp.where(i > 0, 1.0, 0.0)
    has_next = jnp.where(i < pl.num_programs(1) - 1, 1.0, 0.0)
    inside = jnp.where(r < HALO, has_prev, jnp.where(r >= HALO + tm, has_next, 1.0))
    for c in range(CONV_CH // CONV_COLS):
        cols = slice(c * CONV_COLS, (c + 1) * CONV_COLS)
        xe = _dot(xn, wx_ref[:, cols]) * inside
        acc = cb_ref[:, cols] + cw_ref[CONV_PAD:CONV_PAD + 1, cols] * xe
        for w in range(CONV_WIDTH):
            if w != CONV_PAD:
                acc = acc + cw_ref[w:w + 1, cols] * pltpu.roll(xe, (CONV_PAD - w) % rows, axis=0)
        y = acc[mid]
        xbc_ref[:, cols] = (y * _sigmoid(y)).astype(BF16)


def _inproj(x, g, wu, wz, wx, wdt, wdtt, cs, cw, cb, tm):
    bsz, seqlen, _ = x.shape
    t = bsz * seqlen
    nt = seqlen // tm
    r = tm // HALO
    nh = seqlen // HALO
    full = lambda a: pl.BlockSpec(a.shape, lambda b, i: (0,) * a.ndim)
    row = lambda w: pl.BlockSpec((tm, w), lambda b, i: (b * nt + i, 0))
    return pl.pallas_call(
        _inproj_kernel,
        grid=(bsz, nt),
        in_specs=[pl.BlockSpec((1, HALO, D_MODEL), lambda b, i: (b, jnp.maximum(i * r - 1, 0), 0)),
                  pl.BlockSpec((1, tm, D_MODEL), lambda b, i: (b, i, 0)),
                  pl.BlockSpec((1, HALO, D_MODEL), lambda b, i: (b, jnp.minimum((i + 1) * r, nh - 1), 0)),
                  full(g), full(wu), full(wz), full(wx), full(wdt), full(wdtt), full(cs), full(cw),
                  full(cb)],
        out_specs=[row(2 * FOURIER_W), row(SSD_W), row(CONV_CH), row(2 * LANES),
                   pl.BlockSpec((2 * LANES, tm), lambda b, i: (0, b * nt + i))],
        out_shape=[jax.ShapeDtypeStruct((t, 2 * FOURIER_W), BF16),
                   jax.ShapeDtypeStruct((t, SSD_W), BF16),
                   jax.ShapeDtypeStruct((t, CONV_CH), BF16),
                   jax.ShapeDtypeStruct((t, 2 * LANES), F32),
                   jax.ShapeDtypeStruct((2 * LANES, t), F32)],
        compiler_params=_cparams(("parallel", "parallel")),
        name="inproj",
    )(x, x, x, g, wu, wz, wx, wdt, wdtt, cs, cw, cb)


def _fourier_kernel(c_ref, s_ref, ab_ref, g_ref, o_ref, acc_ref, *, scale, nb):
    k = pl.program_id(2)

    @pl.when(k == 0)
    def _():
        acc_ref[...] = jnp.zeros_like(acc_ref)

    c = c_ref[...]
    s = s_ref[...]
    for b in range(nb):
        acc_ref[b] += _dot(c, ab_ref[b, :, :FOURIER_W]) + _dot(s, ab_ref[b, :, FOURIER_W:])

    @pl.when(k == pl.num_programs(2) - 1)
    def _():
        for b in range(nb):
            o_ref[b] = _rms(acc_ref[b] * scale, g_ref[...]).astype(BF16)


def _fourier(ab, cos_t, nsin_t, g, nb, tr, tk):
    bsz, seqlen, _ = ab.shape
    scale = float(1.0 / np.sqrt(seqlen * FOURIER_GROUP_DIM))
    return pl.pallas_call(
        functools.partial(_fourier_kernel, scale=scale, nb=nb),
        grid=(bsz // nb, seqlen // tr, seqlen // tk),
        in_specs=[pl.BlockSpec((tr, tk), lambda b, i, k: (i, k)),
                  pl.BlockSpec((tr, tk), lambda b, i, k: (i, k)),
                  pl.BlockSpec((nb, tk, 2 * FOURIER_W), lambda b, i, k: (b, k, 0)),
                  pl.BlockSpec((1, FOURIER_W), lambda b, i, k: (0, 0))],
        out_specs=pl.BlockSpec((nb, tr, FOURIER_W), lambda b, i, k: (b, i, 0)),
        out_shape=jax.ShapeDtypeStruct((bsz, seqlen, FOURIER_W), BF16),
        scratch_shapes=[pltpu.VMEM((nb, tr, FOURIER_W), F32)],
        compiler_params=_cparams(("parallel", "parallel", "arbitrary")),
        name="fourier",
    )(cos_t, nsin_t, ab, g)


def _dft_tables(seqlen):
    lo = min(seqlen, LANES)
    hi = seqlen // lo
    k = np.arange(seqlen, dtype=np.int64)
    a_hi = 2.0 * np.pi * ((np.arange(hi, dtype=np.int64)[:, None] * lo * k[None, :]) % seqlen) / seqlen
    a_lo = 2.0 * np.pi * ((np.arange(lo, dtype=np.int64)[:, None] * k[None, :]) % seqlen) / seqlen
    c1 = jnp.asarray(np.cos(a_hi), F32)[:, None, :]
    s1 = jnp.asarray(np.sin(a_hi), F32)[:, None, :]
    c0 = jnp.asarray(np.cos(a_lo), F32)[None, :, :]
    s0 = jnp.asarray(np.sin(a_lo), F32)[None, :, :]
    cos_t = (c1 * c0 - s1 * s0).reshape(seqlen, seqlen).astype(BF16)
    nsin_t = (-(s1 * c0 + c1 * s0)).reshape(seqlen, seqlen).astype(BF16)
    return cos_t, nsin_t


def _channel_dft_table():
    c = np.arange(FOURIER_W)
    same = (c[:, None] // FOURIER_GROUP_DIM) == (c[None, :] // FOURIER_GROUP_DIM)
    ang = 2.0 * np.pi * ((c[:, None] % FOURIER_GROUP_DIM) * (c[None, :] % FOURIER_GROUP_DIM)
                         % FOURIER_GROUP_DIM) / FOURIER_GROUP_DIM
    cc = np.where(same, np.cos(ang), 0.0)
    sc = np.where(same, np.sin(ang), 0.0)
    return jnp.asarray(np.concatenate([cc, sc], axis=1), BF16)


def _ssd_kernel(xf_ref, xb_ref, dtf_ref, dtb_ref, dttf_ref, dttb_ref, tri_ref, alog_ref, alogt_ref,
                bias_ref, biast_ref, dsk_ref, yf_ref, yb_ref, state_ref):
    @pl.when(pl.program_id(1) == 0)
    def _():
        state_ref[...] = jnp.zeros_like(state_ref)

    for d, (x_ref, dt_ref, dtt_ref, y_ref) in enumerate(((xf_ref, dtf_ref, dttf_ref, yf_ref),
                                                         (xb_ref, dtb_ref, dttb_ref, yb_ref))):
        _ssd_chunk(x_ref, dt_ref, dtt_ref, tri_ref.at[d], tri_ref.at[1 - d], alog_ref.at[d],
                   alogt_ref.at[d], bias_ref.at[d], biast_ref.at[d], dsk_ref.at[d], y_ref,
                   state_ref.at[d])


def _ssd_chunk(xbc_ref, dt_ref, dtt_ref, tri_ref, trit_ref, alog_ref, alogt_ref, dtb_ref, dtbt_ref,
               dsk_ref, y_ref, state_ref):
    q = CHUNK
    tri = tri_ref[...]
    mask = tri > 0.5
    dt = _softplus(dt_ref[0] + dtb_ref[...])
    d0, d1, d2 = _split3(dt * -jnp.exp(alog_ref[...]))
    tri_b = tri.astype(BF16)
    acum2 = (_dot(tri_b, d0) + _dot(tri_b, d1) + _dot(tri_b, d2)) * LOG2E
    dt_t = _softplus(dtt_ref[...] + dtbt_ref[...])
    t0, t1, t2 = _split3(dt_t * -jnp.exp(alogt_ref[...]))
    trit_b = trit_ref[...].astype(BF16)
    ones_b = jnp.ones((q, LANES), BF16)
    acum_t2 = (_dot(t0, trit_b) + _dot(t1, trit_b) + _dot(t2, trit_b)) * LOG2E
    tot2 = (_dot(t0, ones_b) + _dot(t1, ones_b) + _dot(t2, ones_b)) * LOG2E
    row2 = acum_t2 - jnp.log2(dt_t)
    w_t = dt_t * jnp.exp2(tot2 - acum_t2)
    etot = jnp.exp2(tot2)
    lane_lo = lax.broadcasted_iota(jnp.int32, (1, LANES), 1) < SSD_HEAD_DIM

    for g in range(SSD_GROUPS):
        bm = xbc_ref[0, :, SSD_W + g * D_STATE:SSD_W + (g + 1) * D_STATE]
        cm = xbc_ref[0, :, SSD_W + GN + g * D_STATE:SSD_W + GN + (g + 1) * D_STATE]
        cb = _dot_nt(cm, bm)
        cm_f = cm.astype(F32)
        bm_t = bm.astype(F32).T
        for p in range(SSD_HPG // 2):
            t = g * (SSD_HPG // 2) + p
            cols = slice(t * LANES, (t + 1) * LANES)
            xs = xbc_ref[0, :, cols]
            st = state_ref[:, cols]
            lhs = []
            lhs2 = []
            for h in (2 * t, 2 * t + 1):
                colb2 = jnp.broadcast_to(acum2[:, h:h + 1], (q, q))
                decay_dt = jnp.where(mask, jnp.exp2(colb2 - row2[h:h + 1, :]), 0.0)
                lhs.append(jnp.concatenate([(cb * decay_dt).astype(BF16),
                                            (cm_f * jnp.exp2(colb2)).astype(BF16)], axis=1))
                lhs2.append((bm_t * w_t[h:h + 1, :]).astype(BF16))
            yy = _dot(jnp.concatenate(lhs, axis=0), jnp.concatenate([xs, st.astype(BF16)], axis=0))
            y = jnp.where(lane_lo, yy[:q], yy[q:]) + dsk_ref[:, cols] * xs.astype(F32)
            y_ref[0, :, cols] = y.astype(BF16)
            dec = jnp.where(lane_lo, etot[2 * t:2 * t + 1, :], etot[2 * t + 1:2 * t + 2, :])
            ss = _dot(jnp.concatenate(lhs2, axis=0), xs)
            state_ref[:, cols] = st * dec + jnp.where(lane_lo, ss[:D_STATE], ss[D_STATE:])


def _ssd(xbc, dt, dtt, tri, alog, alogt, dtb, dtbt, dsk):
    bsz, seqlen, _ = xbc.shape
    nc = seqlen // CHUNK
    full = lambda a: pl.BlockSpec(a.shape, lambda b, j: (0,) * a.ndim)
    return pl.pallas_call(
        _ssd_kernel,
        grid=(bsz, nc),
        in_specs=[pl.BlockSpec((1, CHUNK, CONV_CH), lambda b, j: (b, j, 0)),
                  pl.BlockSpec((1, CHUNK, CONV_CH), lambda b, j: (b, nc - 1 - j, 0)),
                  pl.BlockSpec((1, CHUNK, LANES), lambda b, j: (b, j, 0)),
                  pl.BlockSpec((1, CHUNK, LANES), lambda b, j: (b, nc - 1 - j, 1)),
                  pl.BlockSpec((LANES, CHUNK), lambda b, j: (0, b * nc + j)),
                  pl.BlockSpec((LANES, CHUNK), lambda b, j: (1, b * nc + nc - 1 - j)),
                  full(tri), full(alog), full(alogt), full(dtb), full(dtbt), full(dsk)],
        out_specs=[pl.BlockSpec((1, CHUNK, SSD_W), lambda b, j: (b, j, 0)),
                   pl.BlockSpec((1, CHUNK, SSD_W), lambda b, j: (b, nc - 1 - j, 0))],
        out_shape=[jax.ShapeDtypeStruct((bsz, seqlen, SSD_W), BF16)] * 2,
        scratch_shapes=[pltpu.VMEM((2, D_STATE, SSD_W), F32)],
        compiler_params=_cparams(("parallel", "arbitrary")),
        name="ssd",
    )(xbc, xbc, dt, dt, dtt, dtt, tri, alog, alogt, dtb, dtbt, dsk)


def _kv_kernel(mem_ref, g_ref, wk_ref, wv_ref, k_ref, v_ref):
    mn = _rms(mem_ref[0], g_ref[...]).astype(BF16)
    k_ref[0] = _dot(mn, wk_ref[...]).astype(BF16)
    v_ref[0] = _dot(mn, wv_ref[...]).astype(BF16)


def _kv(mem, g, wk, wv):
    bsz, nm, _ = mem.shape
    full = lambda a: pl.BlockSpec(a.shape, lambda b: (0,) * a.ndim)
    blk = pl.BlockSpec((1, nm, D_MODEL), lambda b: (b, 0, 0))
    return pl.pallas_call(
        _kv_kernel,
        grid=(bsz,),
        in_specs=[blk, full(g), full(wk), full(wv)],
        out_specs=[blk, blk],
        out_shape=[jax.ShapeDtypeStruct(mem.shape, BF16)] * 2,
        compiler_params=_cparams(("parallel",)),
        name="kv",
    )(mem, g, wk, wv)


def _cross_attend(h, g, wq_ref, k_ref, v_ref, wo_ref):
    hn = _rms(h, g).astype(BF16)
    q = (_dot(hn, wq_ref[...]) * (XATTN_HEAD_DIM ** -0.5)).astype(BF16)
    outs = []
    for hd in range(XATTN_HEADS):
        cols = slice(hd * XATTN_HEAD_DIM, (hd + 1) * XATTN_HEAD_DIM)
        s = _dot_nt(q[:, cols], k_ref[0, :, cols])
        e = jnp.exp(s - jnp.max(s, axis=-1, keepdims=True))
        p = (e / jnp.sum(e, axis=-1, keepdims=True)).astype(BF16)
        outs.append(_dot(p, v_ref[0, :, cols]).astype(BF16))
    return h + _dot(jnp.concatenate(outs, axis=1), wo_ref[...])


def _route_tile(xn, i, cols, w0, w1, br, rloc_ref, gate_ref, carry_ref, cnt_acc, car_acc):
    tm = xn.shape[0]
    x0, x1, _ = _split3(xn)
    logits = _dot_nt(w0, x0) + _dot_nt(w0, x1) + _dot_nt(w1, x0) + br

    eid = lax.broadcasted_iota(jnp.int32, (N_EXPERTS, tm), 0)
    work = logits
    vals, hits = [], []
    sel = jnp.zeros((N_EXPERTS, tm), F32)
    for k in range(TOP_K):
        m = jnp.max(work, axis=0, keepdims=True)
        ik = jnp.min(jnp.where(work == m, eid, N_EXPERTS), axis=0, keepdims=True)
        hit = eid == ik
        vals.append(m)
        hits.append(hit)
        sel = jnp.where(hit, 1.0, sel)
        work = jnp.where(hit, -jnp.inf, work)
    es = [jnp.exp(v - vals[0]) for v in vals]
    den = es[0] + es[1] + es[2] + es[3]
    for k in range(TOP_K):
        gate_ref[k:k + 1, cols] = es[k] / den

    r = lax.broadcasted_iota(jnp.int32, (tm, tm), 0)
    c = lax.broadcasted_iota(jnp.int32, (tm, tm), 1)
    before = jnp.where(r < c, 1.0, 0.0).astype(BF16)
    cum = _dot(sel.astype(BF16), before)
    counts = jnp.sum(sel, axis=1, keepdims=True)
    counts = jnp.floor((counts + (RUN_ALIGN - 1)) * (1.0 / RUN_ALIGN)) * RUN_ALIGN
    er = lax.broadcasted_iota(jnp.int32, (N_EXPERTS, N_EXPERTS), 0)
    ec = lax.broadcasted_iota(jnp.int32, (N_EXPERTS, N_EXPERTS), 1)
    lower = jnp.where(er > ec, 1.0, 0.0).astype(BF16)
    offs = _dot(lower, jnp.broadcast_to(counts, (N_EXPERTS, LANES)).astype(BF16))[:, 0:1]
    pos = cum + offs
    for k in range(TOP_K):
        rk = jnp.sum(jnp.where(hits[k], pos, 0.0), axis=0, keepdims=True)
        rloc_ref[k:k + 1, cols] = rk.astype(jnp.int32)

    lane = lax.broadcasted_iota(jnp.int32, cnt_acc.shape, 1)
    cnt_acc[...] = jnp.where(lane == i, counts, cnt_acc[...])
    car_acc[...] = jnp.where(lane == i, carry_ref[:, 0:1], car_acc[...])
    carry_ref[...] = carry_ref[...] + counts


def _post_kernel(x_ref, four_ref, yf_ref, yb_ref, z_ref, gs_ref, w4_ref, wy_ref, gx_ref, wq_ref, k_ref, v_ref,
                 wo_ref, gm_ref, wr_ref, br_ref, h_ref, xn_ref, rloc_ref, gate_ref, cnt_ref, car_ref,
                 carry_ref, cnt_acc, car_acc):
    step = pl.program_id(0) * pl.num_programs(1) + pl.program_id(1)
    tm = x_ref.shape[1]

    @pl.when(step == 0)
    def _():
        carry_ref[...] = jnp.zeros_like(carry_ref)
        cnt_acc[...] = jnp.zeros_like(cnt_acc)
        car_acc[...] = jnp.zeros_like(car_acc)

    y = yf_ref[0].astype(F32) + yb_ref[0].astype(F32)
    z = z_ref[0].astype(F32)
    yn = _rms(y * (z * _sigmoid(z)), gs_ref[...]).astype(BF16)
    h = x_ref[0] + _dot(four_ref[0], w4_ref[...]) + _dot(yn, wy_ref[...])

    h = _cross_attend(h, gx_ref[...], wq_ref, k_ref, v_ref, wo_ref)
    h_ref[0] = h

    xn = _rms(h, gm_ref[...])
    xn_ref[0] = xn.astype(BF16)
    w0, w1, _ = _split3(wr_ref[...])
    sub = tm // MOE_TILE
    for s in range(sub):
        rows = slice(s * MOE_TILE, (s + 1) * MOE_TILE)
        _route_tile(xn[rows], step * sub + s, rows, w0, w1, br_ref[...], rloc_ref, gate_ref,
                    carry_ref, cnt_acc, car_acc)
    cnt_ref[...] = cnt_acc[...].astype(jnp.int32)
    car_ref[...] = car_acc[...].astype(jnp.int32)


def _post(x, four, yf, yb, z, mem_k, mem_v, p, tm):
    bsz, seqlen, _ = x.shape
    t = bsz * seqlen
    nt = seqlen // tm
    n_tiles = t // MOE_TILE
    nm = mem_k.shape[1]
    weights = [p["g_ssd"], p["w4"], p["wy"], p["g_x"], p["wq"]]
    weights2 = [p["wo"], p["g_moe"], p["wr_t"], p["br"]]
    full = lambda a: pl.BlockSpec(a.shape, lambda b, i: (0,) * a.ndim)
    tok = lambda w: pl.BlockSpec((1, tm, w), lambda b, i: (b, i, 0))
    mem = pl.BlockSpec((1, nm, D_MODEL), lambda b, i: (b, 0, 0))
    kt = pl.BlockSpec((TOP_K, tm), lambda b, i: (0, b * nt + i))
    per_tile = pl.BlockSpec((N_EXPERTS, n_tiles), lambda b, i: (0, 0))
    return pl.pallas_call(
        _post_kernel,
        grid=(bsz, nt),
        in_specs=[tok(D_MODEL), tok(FOURIER_W),
                  tok(SSD_W), tok(SSD_W), tok(SSD_W)]
                 + [full(a) for a in weights] + [mem, mem] + [full(a) for a in weights2],
        out_specs=[tok(D_MODEL), tok(D_MODEL), kt, kt, per_tile, per_tile],
        out_shape=[jax.ShapeDtypeStruct((bsz, seqlen, D_MODEL), F32),
                   jax.ShapeDtypeStruct((bsz, seqlen, D_MODEL), BF16),
                   jax.ShapeDtypeStruct((TOP_K, t), jnp.int32),
                   jax.ShapeDtypeStruct((TOP_K, t), F32),
                   jax.ShapeDtypeStruct((N_EXPERTS, n_tiles), jnp.int32),
                   jax.ShapeDtypeStruct((N_EXPERTS, n_tiles), jnp.int32)],
        scratch_shapes=[pltpu.VMEM((N_EXPERTS, LANES), F32),
                        pltpu.VMEM((N_EXPERTS, n_tiles), F32),
                        pltpu.VMEM((N_EXPERTS, n_tiles), F32)],
        compiler_params=_cparams(("arbitrary", "arbitrary")),
        name="post",
    )(x, four, yf, yb, z, *weights, mem_k, mem_v, *weights2)


def _pow2_pieces(n, max_size, piece_fn):
    done = jnp.int32(0)
    size = max_size
    while size >= RUN_ALIGN:
        bit = n & size

        @pl.when(bit != 0)
        def _(size=size, done=done):
            piece_fn(done, size)

        done = done + bit
        size //= 2


def _for_each_run(tile, cnt_ref, car_ref, pstart_ref, tile_rows, piece_fn):
    def per_expert(e, local):
        n = cnt_ref[tile * N_EXPERTS + e]
        local = pl.multiple_of(local, RUN_ALIGN)
        slot = pl.multiple_of(pstart_ref[e] + car_ref[tile * N_EXPERTS + e], RUN_ALIGN)
        _pow2_pieces(n, tile_rows, lambda done, size: piece_fn(
            pl.multiple_of(local + done, RUN_ALIGN), pl.multiple_of(slot + done, RUN_ALIGN), size))
        return local + n

    return lax.fori_loop(0, N_EXPERTS, per_expert, jnp.int32(0))


def _sorted_rows(tile_rows):
    return TOP_K * tile_rows + N_EXPERTS * RUN_ALIGN


def _sorted_onehot(rloc_ref, tile_rows):
    rows = lax.broadcasted_iota(jnp.int32, (_sorted_rows(tile_rows), tile_rows), 0)
    return [rows == rloc_ref[k:k + 1, :] for k in range(TOP_K)]


def _dispatch_kernel(pstart_ref, cnt_ref, car_ref, zstart_ref, nzero_ref, x_ref, rloc_ref, xs_hbm,
                     ybuf, zbuf, total_s, sem_z, sem_y):
    i = pl.program_id(0)
    td = x_ref.shape[0]
    slot = i % 2

    def wait_slot(s):
        def wait_piece(done, size):
            del done
            pltpu.make_async_copy(ybuf.at[s, pl.ds(0, size)], xs_hbm.at[pl.ds(0, size)],
                                  sem_y.at[s]).wait()

        _pow2_pieces(total_s[s], TOP_K * td, wait_piece)

    @pl.when(i == 0)
    def _():
        zbuf[...] = jnp.zeros_like(zbuf)

        def zcopy(start):
            dst = pl.ds(pl.multiple_of(start, MOE_BLK), MOE_BLK)
            return pltpu.make_async_copy(zbuf, xs_hbm.at[dst], sem_z)

        def issue(j, carry):
            zcopy(zstart_ref[j]).start()
            return carry

        def issue_tail(j, carry):
            zcopy(j * MOE_BLK).start()
            return carry

        def drain(j, carry):
            zcopy(0).wait()
            return carry

        n_blocks = xs_hbm.shape[0] // MOE_BLK
        lax.fori_loop(0, nzero_ref[0], issue, 0)
        lax.fori_loop(nzero_ref[1], n_blocks, issue_tail, 0)
        lax.fori_loop(0, nzero_ref[0] + (n_blocks - nzero_ref[1]), drain, 0)

    @pl.when(i >= 2)
    def _():
        wait_slot(slot)

    onehot = jnp.zeros((_sorted_rows(td), td), F32)
    for hit in _sorted_onehot(rloc_ref, td):
        onehot = jnp.where(hit, 1.0, onehot)
    ybuf[slot] = _pack_halves(_dot(onehot.astype(BF16), x_ref[...]))

    def send(local, dst, size):
        pltpu.make_async_copy(ybuf.at[slot, pl.ds(local, size)], xs_hbm.at[pl.ds(dst, size)],
                              sem_y.at[slot]).start()

    total_s[slot] = _for_each_run(i, cnt_ref, car_ref, pstart_ref, td, send)

    @pl.when(i == pl.num_programs(0) - 1)
    def _():
        wait_slot(slot)

        @pl.when(i >= 1)
        def _():
            wait_slot(1 - slot)


def _dispatch(pstart, cnt, car, zstart, nzero, xn, rloc, n_slots, td):
    t = xn.shape[0]
    half = D_MODEL // 2
    return pl.pallas_call(
        _dispatch_kernel,
        grid_spec=pltpu.PrefetchScalarGridSpec(
            num_scalar_prefetch=5,
            grid=(t // td,),
            in_specs=[pl.BlockSpec((td, D_MODEL), lambda i, *_: (i, 0)),
                      pl.BlockSpec((TOP_K, td), lambda i, *_: (0, i))],
            out_specs=pl.BlockSpec(memory_space=pl.ANY),
            scratch_shapes=[pltpu.VMEM((2, _sorted_rows(td), half), jnp.uint32),
                            pltpu.VMEM((MOE_BLK, half), jnp.uint32),
                            pltpu.SMEM((2,), jnp.int32),
                            pltpu.SemaphoreType.DMA,
                            pltpu.SemaphoreType.DMA((2,))]),
        out_shape=jax.ShapeDtypeStruct((n_slots, half), jnp.uint32),
        compiler_params=_cparams(("arbitrary",)),
        name="dispatch",
    )(pstart, cnt, car, zstart, nzero, xn, rloc)


def _expert_kernel(be_ref, na_ref, xs_ref, w1_ref, b1_ref, w2_ref, b2_ref, o_ref):
    i = pl.program_id(0)

    @pl.when(i < na_ref[0])
    def _():
        h = _dot(_unpack_halves(xs_ref[...]), w1_ref[0]) + b1_ref[0]
        glu = jnp.minimum(h[:, :D_FF], SWIGLU_LIMIT)
        lin = jnp.clip(h[:, D_FF:], -SWIGLU_LIMIT, SWIGLU_LIMIT)
        act = glu * _sigmoid(SWIGLU_ALPHA * glu) * (lin + 1.0)
        o = (_dot(act.astype(BF16), w2_ref[0]) + b2_ref[0]).astype(BF16).astype(F32)
        o_ref[...] = _pack_halves(o)

    @pl.when(i >= na_ref[0])
    def _():
        o_ref[...] = jnp.zeros_like(o_ref)


def _experts(block_e, n_active, xs, w1, b1, w2, b2):
    n_slots = xs.shape[0]
    return pl.pallas_call(
        _expert_kernel,
        grid_spec=pltpu.PrefetchScalarGridSpec(
            num_scalar_prefetch=2,
            grid=(n_slots // MOE_BLK,),
            in_specs=[pl.BlockSpec((MOE_BLK, D_MODEL // 2),
                                   lambda i, be, na: (jnp.minimum(i, na[0] - 1), 0)),
                      pl.BlockSpec((1, D_MODEL, 2 * D_FF), lambda i, be, na: (be[i], 0, 0)),
                      pl.BlockSpec((1, 1, 2 * D_FF), lambda i, be, na: (be[i], 0, 0)),
                      pl.BlockSpec((1, D_FF, D_MODEL), lambda i, be, na: (be[i], 0, 0)),
                      pl.BlockSpec((1, 1, D_MODEL), lambda i, be, na: (be[i], 0, 0))],
            out_specs=pl.BlockSpec((MOE_BLK, D_MODEL // 2), lambda i, be, na: (i, 0))),
        out_shape=jax.ShapeDtypeStruct((n_slots, D_MODEL // 2), jnp.uint32),
        compiler_params=_cparams(("arbitrary",)),
        name="experts",
    )(block_e, n_active, xs, w1, b1, w2, b2)


def _combine_kernel(pstart_ref, cnt_ref, car_ref, h_ref, rloc_ref, gate_ref, gf_ref, outs_hbm, o_ref,
                    ybuf, total_s, sem_y):
    i = pl.program_id(0)
    nt = pl.num_programs(0)
    tc = h_ref.shape[0]
    slot = i % 2

    def fetch(tile, s):
        def piece(local, src, size):
            pltpu.make_async_copy(outs_hbm.at[pl.ds(src, size)], ybuf.at[s, pl.ds(local, size)],
                                  sem_y.at[s]).start()

        total_s[s] = _for_each_run(tile, cnt_ref, car_ref, pstart_ref, tc, piece)

    @pl.when(i == 0)
    def _():
        fetch(i, slot)

    @pl.when(i + 1 < nt)
    def _():
        fetch(i + 1, 1 - slot)

    hits = _sorted_onehot(rloc_ref, tc)
    gsel = jnp.zeros((_sorted_rows(tc), tc), F32)
    for k in range(TOP_K):
        gsel = jnp.where(hits[k], gate_ref[k:k + 1, :], gsel)
    g_t = gsel.T
    g0 = g_t.astype(BF16)
    g1 = (g_t - g0.astype(F32)).astype(BF16)

    def wait_piece(done, size):
        del done
        pltpu.make_async_copy(outs_hbm.at[pl.ds(0, size)], ybuf.at[slot, pl.ds(0, size)],
                              sem_y.at[slot]).wait()

    _pow2_pieces(total_s[slot], TOP_K * tc, wait_piece)
    row = lax.broadcasted_iota(jnp.int32, (_sorted_rows(tc), 1), 0)
    yb = _unpack_halves(jnp.where(row < total_s[slot], ybuf[slot], jnp.uint32(0)))
    o_ref[...] = _rms(h_ref[...] + _dot(g0, yb) + _dot(g1, yb), gf_ref[...])


def _combine(pstart, cnt, car, h2d, rloc, gate, gf, outs, tc):
    t = h2d.shape[0]
    kt = pl.BlockSpec((TOP_K, tc), lambda i, *_: (0, i))
    return pl.pallas_call(
        _combine_kernel,
        grid_spec=pltpu.PrefetchScalarGridSpec(
            num_scalar_prefetch=3,
            grid=(t // tc,),
            in_specs=[pl.BlockSpec((tc, D_MODEL), lambda i, *_: (i, 0)), kt, kt,
                      pl.BlockSpec((1, D_MODEL), lambda i, *_: (0, 0)),
                      pl.BlockSpec(memory_space=pl.ANY)],
            out_specs=pl.BlockSpec((tc, D_MODEL), lambda i, *_: (i, 0)),
            scratch_shapes=[pltpu.VMEM((2, _sorted_rows(tc), D_MODEL // 2), jnp.uint32),
                            pltpu.SMEM((2,), jnp.int32),
                            pltpu.SemaphoreType.DMA((2,))]),
        out_shape=jax.ShapeDtypeStruct((t, D_MODEL), F32),
        compiler_params=_cparams(("arbitrary",)),
        name="combine",
    )(pstart, cnt, car, h2d, rloc, gate, gf, outs)


def _prep_weights(g_mix_norm, w_in, conv_w, conv_b, a_log_fwd, a_log_bwd, dt_bias_fwd, dt_bias_bwd,
                  d_skip, g_fourier_norm, g_ssd_norm, w_out, g_xattn_norm, g_mem_norm, w_q, w_k,
                  w_v, w_o, g_moe_norm, w_router, b_router, w_mlp1, b_mlp1, w_mlp2, b_mlp2, g_final):
    s1 = FOURIER_W
    s2 = s1 + SSD_W
    s3 = s2 + CONV_CH
    w = w_in[0]
    wdt = jnp.zeros((D_MODEL, 2 * LANES), F32)
    wdt = wdt.at[:, :SSD_HEADS].set(w[:, s3:s3 + SSD_HEADS])
    wdt = wdt.at[:, LANES:LANES + SSD_HEADS].set(w[:, s3 + SSD_HEADS:])
    pad_heads = lambda a, b: jnp.zeros((2, 1, LANES), F32).at[0, 0, :SSD_HEADS].set(a[0]).at[
        1, 0, :SSD_HEADS].set(b[0])
    head_major = lambda a: jnp.broadcast_to(jnp.swapaxes(a, 1, 2), (2, LANES, LANES))
    alog = pad_heads(a_log_fwd, a_log_bwd)
    dtb = pad_heads(dt_bias_fwd, dt_bias_bwd)
    dsk = jnp.zeros((2, 1, SSD_W), F32).at[0, 0].set(jnp.repeat(d_skip[0], SSD_HEAD_DIM))
    lower = np.tril(np.ones((CHUNK, CHUNK), np.float32))
    return dict(
        g_mix=g_mix_norm[0][None], wu=w[:, :s1].astype(BF16), wz=w[:, s1:s2].astype(BF16),
        wx=w[:, s2:s3].astype(BF16), wdt=wdt.astype(BF16), wdtt=wdt.T.astype(BF16),
        cs=_channel_dft_table(), conv_w=conv_w[0, :, 0, :], conv_b=conv_b[0][None],
        tri=jnp.asarray(np.stack([lower, lower.T])), alog=alog, alogt=head_major(alog),
        dtb=dtb, dtbt=head_major(dtb), dsk=dsk,
        g_four=g_fourier_norm[0][None], g_ssd=g_ssd_norm[0][None],
        w4=w_out[0, :FOURIER_W].astype(BF16), wy=w_out[0, FOURIER_W:].astype(BF16),
        g_x=g_xattn_norm[0][None], g_mem=g_mem_norm[0][None],
        wq=w_q[0].astype(BF16), wk=w_k[0].astype(BF16), wv=w_v[0].astype(BF16), wo=w_o[0].astype(BF16),
        g_moe=g_moe_norm[0][None], wr_t=w_router[0].T, br=b_router[0][:, None],
        w1=w_mlp1[0].astype(BF16), b1=b_mlp1[0][:, None, :], w2=w_mlp2[0].astype(BF16),
        b2=b_mlp2[0][:, None, :], g_final=g_final[None])


def _encode_group(x, mem, p):
    bsz, seqlen, _ = x.shape
    t = bsz * seqlen
    tm = min(TOKEN_TILE, seqlen)

    ab, z, xbc, dt, dtt = _inproj(x, p["g_mix"], p["wu"], p["wz"], p["wx"], p["wdt"], p["wdtt"],
                                  p["cs"], p["conv_w"], p["conv_b"], tm)
    cos_t, nsin_t = _dft_tables(seqlen)
    nb = 4 if bsz % 4 == 0 else (2 if bsz % 2 == 0 else 1)
    tf = min(DFT_TILE, seqlen)
    four = _fourier(ab.reshape(bsz, seqlen, 2 * FOURIER_W), cos_t, nsin_t, p["g_four"], nb, tf, tf)
    yf, yb = _ssd(xbc.reshape(bsz, seqlen, CONV_CH), dt.reshape(bsz, seqlen, 2 * LANES), dtt,
                  p["tri"], p["alog"], p["alogt"], p["dtb"], p["dtbt"], p["dsk"])

    k, v = _kv(mem, p["g_mem"], p["wk"], p["wv"])
    h, xn, rloc, gate, cnt, car = _post(x, four, yf, yb, z.reshape(bsz, seqlen, SSD_W), k, v, p, tm)
    h = h.reshape(t, D_MODEL)
    xn = xn.reshape(t, D_MODEL)

    td = MOE_TILE
    counts = jnp.sum(cnt, axis=1)
    padded = ((counts + MOE_BLK - 1) // MOE_BLK) * MOE_BLK
    pend = jnp.cumsum(padded)
    pstart = (pend - padded).astype(jnp.int32)
    max_rows = t * TOP_K + (t // td) * N_EXPERTS * (RUN_ALIGN - 1)
    n_blocks = -(-max_rows // MOE_BLK) + N_EXPERTS
    n_slots = n_blocks * MOE_BLK
    n_active = (pend[-1] // MOE_BLK).astype(jnp.int32)
    blk = jnp.arange(n_blocks, dtype=jnp.int32)
    be = jnp.sum(pend[None, :] <= (jnp.minimum(blk, n_active - 1) * MOE_BLK)[:, None], axis=1)
    be = jnp.minimum(be, N_EXPERTS - 1).astype(jnp.int32)
    zstart = (pend - MOE_BLK).astype(jnp.int32)[jnp.argsort(padded == 0)]
    nzero = jnp.stack([jnp.sum(padded > 0).astype(jnp.int32), n_active])

    cnt_flat = cnt.T.reshape(-1)
    car_flat = car.T.reshape(-1)
    xs = _dispatch(pstart, cnt_flat, car_flat, zstart, nzero, xn, rloc, n_slots, td)
    outs = _experts(be, n_active[None], xs, p["w1"], p["b1"], p["w2"], p["b2"])
    y = _combine(pstart, cnt_flat, car_flat, h, rloc, gate, p["g_final"], outs, td)
    return y.reshape(bsz, seqlen, D_MODEL)


def kernel(x_prompt, x_sample, mem_prompt, mem_sample, g_mix_norm, w_in, conv_w, conv_b, a_log_fwd, a_log_bwd, dt_bias_fwd, dt_bias_bwd, d_skip, g_fourier_norm, g_ssd_norm, w_out, g_xattn_norm, g_mem_norm, w_q, w_k, w_v, w_o, g_moe_norm, w_router, b_router, w_mlp1, b_mlp1, w_mlp2, b_mlp2, g_final):
    p = _prep_weights(g_mix_norm, w_in, conv_w, conv_b, a_log_fwd, a_log_bwd, dt_bias_fwd,
                      dt_bias_bwd, d_skip, g_fourier_norm, g_ssd_norm, w_out, g_xattn_norm,
                      g_mem_norm, w_q, w_k, w_v, w_o, g_moe_norm, w_router, b_router, w_mlp1,
                      b_mlp1, w_mlp2, b_mlp2, g_final)
    return (_encode_group(x_prompt, mem_prompt, p), _encode_group(x_sample, mem_sample, p))
```

```python
import functools

import numpy as np
import jax
import jax.numpy as jnp
from jax import lax
from jax.experimental import pallas as pl
from jax.experimental.pallas import tpu as pltpu

F32 = jnp.float32
BF16 = jnp.bfloat16

D_MODEL = 1024
D_MIX = 2 * D_MODEL
FOURIER_W = D_MIX // 4
FOURIER_GROUPS = 8
FOURIER_GROUP_DIM = FOURIER_W // FOURIER_GROUPS
SSD_W = D_MIX - FOURIER_W
SSD_HEAD_DIM = 64
SSD_HEADS = SSD_W // SSD_HEAD_DIM
SSD_GROUPS = 4
SSD_HPG = SSD_HEADS // SSD_GROUPS
D_STATE = 128
CONV_WIDTH = 5
CONV_PAD = CONV_WIDTH // 2
GN = SSD_GROUPS * D_STATE
CONV_CH = SSD_W + 2 * GN
CHUNK = 128
XATTN_HEADS = 4
XATTN_HEAD_DIM = D_MODEL // XATTN_HEADS
N_EXPERTS = 32
TOP_K = 4
D_FF = D_MODEL
SWIGLU_ALPHA = 1.702
SWIGLU_LIMIT = 7.0
NORM_EPS = 1e-5
LOG2E = 1.4426950408889634

LANES = 128
HALO = 8
CONV_COLS = 512
TOKEN_TILE = 512
DFT_TILE = 1024
CARRY_ROWS = 16
MOE_BLK = 512
MOE_TILE = 256
RUN_ALIGN = 8
VMEM_LIMIT = 48 * 1024 * 1024


def _cparams(sem):
    return pltpu.CompilerParams(dimension_semantics=sem, vmem_limit_bytes=VMEM_LIMIT)


def _rms(x, g):
    return x * lax.rsqrt(jnp.mean(x * x, axis=-1, keepdims=True) + NORM_EPS) * g


def _sigmoid(x):
    return 1.0 / (1.0 + jnp.exp(-x))


def _softplus(x):
    return jnp.maximum(x, 0.0) + jnp.log1p(jnp.exp(-jnp.abs(x)))


def _dot(a, b):
    return jnp.dot(a, b, preferred_element_type=F32)


def _dot_nt(a, b):
    return lax.dot_general(a, b, (((1,), (1,)), ((), ())), preferred_element_type=F32)


def _pack_halves(x):
    c = x.shape[1] // 2
    u = lax.bitcast_convert_type(x, jnp.uint32)
    return (u[:, :c] >> 16) | (u[:, c:] & jnp.uint32(0xFFFF0000))


def _unpack_halves(w):
    lo = lax.bitcast_convert_type(w << 16, F32)
    hi = lax.bitcast_convert_type(w & jnp.uint32(0xFFFF0000), F32)
    return jnp.concatenate([lo, hi], axis=1).astype(BF16)


def _split3(x):
    h = x.astype(BF16)
    r = x - h.astype(F32)
    m = r.astype(BF16)
    l = (r - m.astype(F32)).astype(BF16)
    return h, m, l


def _inproj_kernel(prev_ref, x_ref, next_ref, g_ref, wu_ref, wz_ref, wx_ref, wdt_ref, wdtt_ref,
                   cs_ref, cw_ref, cb_ref, ab_ref, z_ref, xbc_ref, dt_ref, dtt_ref):
    i = pl.program_id(1)
    tm = x_ref.shape[1]
    rows = tm + 2 * HALO
    mid = slice(HALO, HALO + tm)
    xa = jnp.concatenate([prev_ref[0], x_ref[0], next_ref[0]], axis=0)
    xn = _rms(xa, g_ref[...]).astype(BF16)
    u = _dot(xn, wu_ref[...])[mid]
    ab_ref[...] = _dot(u.astype(BF16), cs_ref[...]).astype(BF16)
    z_ref[...] = _dot(xn, wz_ref[...])[mid].astype(BF16)
    dt_ref[...] = _dot(xn, wdt_ref[...])[mid]
    dtt_ref[...] = _dot_nt(wdtt_ref[...], xn)[:, mid]

    r = lax.broadcasted_iota(jnp.int32, (rows, 1), 0)
    has_prev = jnp.where(i > 0, 1.0, 0.0)
    has_next = jnp.where(i < pl.num_programs(1) - 1, 1.0, 0.0)
    inside = jnp.where(r < HALO, has_prev, jnp.where(r >= HALO + tm, has_next, 1.0))
    for c in range(CONV_CH // CONV_COLS):
        cols = slice(c * CONV_COLS, (c + 1) * CONV_COLS)
        xe = _dot(xn, wx_ref[:, cols]) * inside
        acc = cb_ref[:, cols] + cw_ref[CONV_PAD:CONV_PAD + 1, cols] * xe
        for w in range(CONV_WIDTH):
            if w != CONV_PAD:
                acc = acc + cw_ref[w:w + 1, cols] * pltpu.roll(xe, (CONV_PAD - w) % rows, axis=0)
        y = acc[mid]
        xbc_ref[:, cols] = (y * _sigmoid(y)).astype(BF16)


def _inproj(x, g, wu, wz, wx, wdt, wdtt, cs, cw, cb, tm):
    bsz, seqlen, _ = x.shape
    t = bsz * seqlen
    nt = seqlen // tm
    r = tm // HALO
    nh = seqlen // HALO
    full = lambda a: pl.BlockSpec(a.shape, lambda b, i: (0,) * a.ndim)
    row = lambda w: pl.BlockSpec((tm, w), lambda b, i: (b * nt + i, 0))
    return pl.pallas_call(
        _inproj_kernel,
        grid=(bsz, nt),
        in_specs=[pl.BlockSpec((1, HALO, D_MODEL), lambda b, i: (b, jnp.maximum(i * r - 1, 0), 0)),
                  pl.BlockSpec((1, tm, D_MODEL), lambda b, i: (b, i, 0)),
                  pl.BlockSpec((1, HALO, D_MODEL), lambda b, i: (b, jnp.minimum((i + 1) * r, nh - 1), 0)),
                  full(g), full(wu), full(wz), full(wx), full(wdt), full(wdtt), full(cs), full(cw),
                  full(cb)],
        out_specs=[row(2 * FOURIER_W), row(SSD_W), row(CONV_CH), row(2 * LANES),
                   pl.BlockSpec((2 * LANES, tm), lambda b, i: (0, b * nt + i))],
        out_shape=[jax.ShapeDtypeStruct((t, 2 * FOURIER_W), BF16),
                   jax.ShapeDtypeStruct((t, SSD_W), BF16),
                   jax.ShapeDtypeStruct((t, CONV_CH), BF16),
                   jax.ShapeDtypeStruct((t, 2 * LANES), F32),
                   jax.ShapeDtypeStruct((2 * LANES, t), F32)],
        compiler_params=_cparams(("parallel", "parallel")),
        name="inproj",
    )(x, x, x, g, wu, wz, wx, wdt, wdtt, cs, cw, cb)


def _fourier_kernel(c_ref, s_ref, cm_ref, ab_ref, g_ref, o_ref, accp_ref, accq_ref, accm_ref,
                    carry_ref, *, scale, nb):
    j = pl.program_id(1)
    k = pl.program_id(2)
    tr = c_ref.shape[0]

    @pl.when(k == 0)
    def _():
        accp_ref[...] = jnp.zeros_like(accp_ref)
        accq_ref[...] = jnp.zeros_like(accq_ref)

    @pl.when((k == 0) & (j == 0))
    def _():
        accm_ref[...] = jnp.zeros_like(accm_ref)

    c = c_ref[...]
    s = s_ref[...]
    for b in range(nb):
        accp_ref[b] += _dot(c, ab_ref[b, :, :FOURIER_W])
        accq_ref[b] += _dot(s, ab_ref[b, :, FOURIER_W:])

    @pl.when(j == 0)
    def _():
        for b in range(nb):
            accm_ref[b] += _dot(cm_ref[...], ab_ref[b, :, :FOURIER_W])

    @pl.when(k == pl.num_programs(2) - 1)
    def _():
        g = g_ref[...]
        r = lax.broadcasted_iota(jnp.int32, (tr, tr + CARRY_ROWS), 0)
        col = lax.broadcasted_iota(jnp.int32, (tr, tr + CARRY_ROWS), 1)
        flip = jnp.where(col == jnp.where(r == 0, tr, tr - r), 1.0, 0.0).astype(BF16)
        for b in range(nb):
            @pl.when(j == 0)
            def _():
                carry_ref[b] = _rms(accm_ref[b] * scale, g).astype(BF16)

            p = accp_ref[b] * scale
            q = accq_ref[b] * scale
            o_ref[b, 0] = _rms(p + q, g).astype(BF16)
            m = _rms(p - q, g).astype(BF16)
            o_ref[b, 1] = _dot(flip, jnp.concatenate([m, carry_ref[b]], axis=0)).astype(BF16)
            carry_ref[b] = m[:CARRY_ROWS]


def _fourier(ab, cos_t, nsin_t, cmid_t, g, nb, tr, tk):
    bsz, seqlen, _ = ab.shape
    half_tiles = seqlen // (2 * tr)
    scale = float(1.0 / np.sqrt(seqlen * FOURIER_GROUP_DIM))
    tile = lambda b, j, k: (half_tiles - 1 - j, k)
    return pl.pallas_call(
        functools.partial(_fourier_kernel, scale=scale, nb=nb),
        grid=(bsz // nb, half_tiles, seqlen // tk),
        in_specs=[pl.BlockSpec((tr, tk), tile),
                  pl.BlockSpec((tr, tk), tile),
                  pl.BlockSpec((CARRY_ROWS, tk), lambda b, j, k: (0, k)),
                  pl.BlockSpec((nb, tk, 2 * FOURIER_W), lambda b, j, k: (b, k, 0)),
                  pl.BlockSpec((1, FOURIER_W), lambda b, j, k: (0, 0))],
        out_specs=pl.BlockSpec((nb, 2, tr, FOURIER_W), lambda b, j, k: (b, 0, half_tiles - 1 - j, 0)),
        out_shape=jax.ShapeDtypeStruct((bsz, 2, seqlen // 2, FOURIER_W), BF16),
        scratch_shapes=[pltpu.VMEM((nb, tr, FOURIER_W), F32),
                        pltpu.VMEM((nb, tr, FOURIER_W), F32),
                        pltpu.VMEM((nb, CARRY_ROWS, FOURIER_W), F32),
                        pltpu.VMEM((nb, CARRY_ROWS, FOURIER_W), BF16)],
        compiler_params=_cparams(("parallel", "arbitrary", "arbitrary")),
        name="fourier",
    )(cos_t, nsin_t, cmid_t, ab, g)


def _dft_tables(seqlen):
    lo = min(seqlen // 2, LANES)
    hi = seqlen // 2 // lo
    l = np.arange(seqlen, dtype=np.int64)
    a_hi = 2.0 * np.pi * ((np.arange(hi, dtype=np.int64)[:, None] * lo * l[None, :]) % seqlen) / seqlen
    a_lo = 2.0 * np.pi * ((np.arange(lo, dtype=np.int64)[:, None] * l[None, :]) % seqlen) / seqlen
    c1 = jnp.asarray(np.cos(a_hi), F32)[:, None, :]
    s1 = jnp.asarray(np.sin(a_hi), F32)[:, None, :]
    c0 = jnp.asarray(np.cos(a_lo), F32)[None, :, :]
    s0 = jnp.asarray(np.sin(a_lo), F32)[None, :, :]
    cos_t = (c1 * c0 - s1 * s0).reshape(seqlen // 2, seqlen).astype(BF16)
    nsin_t = (-(s1 * c0 + c1 * s0)).reshape(seqlen // 2, seqlen).astype(BF16)
    cmid = np.zeros((CARRY_ROWS, seqlen), np.float32)
    cmid[0] = 1.0 - 2.0 * (l % 2)
    return cos_t, nsin_t, jnp.asarray(cmid, BF16)


def _channel_dft_table():
    c = np.arange(FOURIER_W)
    same = (c[:, None] // FOURIER_GROUP_DIM) == (c[None, :] // FOURIER_GROUP_DIM)
    ang = 2.0 * np.pi * ((c[:, None] % FOURIER_GROUP_DIM) * (c[None, :] % FOURIER_GROUP_DIM)
                         % FOURIER_GROUP_DIM) / FOURIER_GROUP_DIM
    cc = np.where(same, np.cos(ang), 0.0)
    sc = np.where(same, np.sin(ang), 0.0)
    return jnp.asarray(np.concatenate([cc, sc], axis=1), BF16)


def _ssd_kernel(xf_ref, xb_ref, dtf_ref, dtb_ref, dttf_ref, dttb_ref, tri_ref, alog_ref, alogt_ref,
                bias_ref, biast_ref, dsk_ref, yf_ref, yb_ref, state_ref):
    @pl.when(pl.program_id(1) == 0)
    def _():
        state_ref[...] = jnp.zeros_like(state_ref)

    for d, (x_ref, dt_ref, dtt_ref, y_ref) in enumerate(((xf_ref, dtf_ref, dttf_ref, yf_ref),
                                                         (xb_ref, dtb_ref, dttb_ref, yb_ref))):
        _ssd_chunk(x_ref, dt_ref, dtt_ref, tri_ref.at[d], tri_ref.at[1 - d], alog_ref.at[d],
                   alogt_ref.at[d], bias_ref.at[d], biast_ref.at[d], dsk_ref.at[d], y_ref,
                   state_ref.at[d])


def _ssd_chunk(xbc_ref, dt_ref, dtt_ref, tri_ref, trit_ref, alog_ref, alogt_ref, dtb_ref, dtbt_ref,
               dsk_ref, y_ref, state_ref):
    q = CHUNK
    tri = tri_ref[...]
    mask = tri > 0.5
    dt = _softplus(dt_ref[0] + dtb_ref[...])
    d0, d1, d2 = _split3(dt * -jnp.exp(alog_ref[...]))
    tri_b = tri.astype(BF16)
    acum2 = (_dot(tri_b, d0) + _dot(tri_b, d1) + _dot(tri_b, d2)) * LOG2E
    dt_t = _softplus(dtt_ref[...] + dtbt_ref[...])
    t0, t1, t2 = _split3(dt_t * -jnp.exp(alogt_ref[...]))
    trit_b = trit_ref[...].astype(BF16)
    ones_b = jnp.ones((q, LANES), BF16)
    acum_t2 = (_dot(t0, trit_b) + _dot(t1, trit_b) + _dot(t2, trit_b)) * LOG2E
    tot2 = (_dot(t0, ones_b) + _dot(t1, ones_b) + _dot(t2, ones_b)) * LOG2E
    row2 = acum_t2 - jnp.log2(dt_t)
    w_t = dt_t * jnp.exp2(tot2 - acum_t2)
    etot = jnp.exp2(tot2)
    lane_lo = lax.broadcasted_iota(jnp.int32, (1, LANES), 1) < SSD_HEAD_DIM

    for g in range(SSD_GROUPS):
        bm = xbc_ref[0, :, SSD_W + g * D_STATE:SSD_W + (g + 1) * D_STATE]
        cm = xbc_ref[0, :, SSD_W + GN + g * D_STATE:SSD_W + GN + (g + 1) * D_STATE]
        cb = _dot_nt(cm, bm)
        cm_f = cm.astype(F32)
        bm_t = bm.astype(F32).T
        for p in range(SSD_HPG // 2):
            t = g * (SSD_HPG // 2) + p
            cols = slice(t * LANES, (t + 1) * LANES)
            xs = xbc_ref[0, :, cols]
            st = state_ref[:, cols]
            lhs = []
            lhs2 = []
            for h in (2 * t, 2 * t + 1):
                colb2 = jnp.broadcast_to(acum2[:, h:h + 1], (q, q))
                decay_dt = jnp.where(mask, jnp.exp2(colb2 - row2[h:h + 1, :]), 0.0)
                lhs.append(jnp.concatenate([(cb * decay_dt).astype(BF16),
                                            (cm_f * jnp.exp2(colb2)).astype(BF16)], axis=1))
                lhs2.append((bm_t * w_t[h:h + 1, :]).astype(BF16))
            yy = _dot(jnp.concatenate(lhs, axis=0), jnp.concatenate([xs, st.astype(BF16)], axis=0))
            y = jnp.where(lane_lo, yy[:q], yy[q:]) + dsk_ref[:, cols] * xs.astype(F32)
            y_ref[0, :, cols] = y.astype(BF16)
            dec = jnp.where(lane_lo, etot[2 * t:2 * t + 1, :], etot[2 * t + 1:2 * t + 2, :])
            ss = _dot(jnp.concatenate(lhs2, axis=0), xs)
            state_ref[:, cols] = st * dec + jnp.where(lane_lo, ss[:D_STATE], ss[D_STATE:])


def _ssd(xbc, dt, dtt, tri, alog, alogt, dtb, dtbt, dsk):
    bsz, seqlen, _ = xbc.shape
    nc = seqlen // CHUNK
    full = lambda a: pl.BlockSpec(a.shape, lambda b, j: (0,) * a.ndim)
    return pl.pallas_call(
        _ssd_kernel,
        grid=(bsz, nc),
        in_specs=[pl.BlockSpec((1, CHUNK, CONV_CH), lambda b, j: (b, j, 0)),
                  pl.BlockSpec((1, CHUNK, CONV_CH), lambda b, j: (b, nc - 1 - j, 0)),
                  pl.BlockSpec((1, CHUNK, LANES), lambda b, j: (b, j, 0)),
                  pl.BlockSpec((1, CHUNK, LANES), lambda b, j: (b, nc - 1 - j, 1)),
                  pl.BlockSpec((LANES, CHUNK), lambda b, j: (0, b * nc + j)),
                  pl.BlockSpec((LANES, CHUNK), lambda b, j: (1, b * nc + nc - 1 - j)),
                  full(tri), full(alog), full(alogt), full(dtb), full(dtbt), full(dsk)],
        out_specs=[pl.BlockSpec((1, CHUNK, SSD_W), lambda b, j: (b, j, 0)),
                   pl.BlockSpec((1, CHUNK, SSD_W), lambda b, j: (b, nc - 1 - j, 0))],
        out_shape=[jax.ShapeDtypeStruct((bsz, seqlen, SSD_W), BF16)] * 2,
        scratch_shapes=[pltpu.VMEM((2, D_STATE, SSD_W), F32)],
        compiler_params=_cparams(("parallel", "arbitrary")),
        name="ssd",
    )(xbc, xbc, dt, dt, dtt, dtt, tri, alog, alogt, dtb, dtbt, dsk)


def _kv_kernel(mem_ref, g_ref, wk_ref, wv_ref, k_ref, v_ref):
    mn = _rms(mem_ref[0], g_ref[...]).astype(BF16)
    k_ref[0] = _dot(mn, wk_ref[...]).astype(BF16)
    v_ref[0] = _dot(mn, wv_ref[...]).astype(BF16)


def _kv(mem, g, wk, wv):
    bsz, nm, _ = mem.shape
    full = lambda a: pl.BlockSpec(a.shape, lambda b: (0,) * a.ndim)
    blk = pl.BlockSpec((1, nm, D_MODEL), lambda b: (b, 0, 0))
    return pl.pallas_call(
        _kv_kernel,
        grid=(bsz,),
        in_specs=[blk, full(g), full(wk), full(wv)],
        out_specs=[blk, blk],
        out_shape=[jax.ShapeDtypeStruct(mem.shape, BF16)] * 2,
        compiler_params=_cparams(("parallel",)),
        name="kv",
    )(mem, g, wk, wv)


def _cross_attend(h, g, wq_ref, k_ref, v_ref, wo_ref):
    hn = _rms(h, g).astype(BF16)
    q = (_dot(hn, wq_ref[...]) * (XATTN_HEAD_DIM ** -0.5)).astype(BF16)
    outs = []
    for hd in range(XATTN_HEADS):
        cols = slice(hd * XATTN_HEAD_DIM, (hd + 1) * XATTN_HEAD_DIM)
        s = _dot_nt(q[:, cols], k_ref[0, :, cols])
        e = jnp.exp(s - jnp.max(s, axis=-1, keepdims=True))
        p = (e / jnp.sum(e, axis=-1, keepdims=True)).astype(BF16)
        outs.append(_dot(p, v_ref[0, :, cols]).astype(BF16))
    return h + _dot(jnp.concatenate(outs, axis=1), wo_ref[...])


def _route_tile(xn, i, cols, w0, w1, br, rloc_ref, gate_ref, carry_ref, cnt_acc, car_acc):
    tm = xn.shape[0]
    x0, x1, _ = _split3(xn)
    logits = _dot_nt(w0, x0) + _dot_nt(w0, x1) + _dot_nt(w1, x0) + br

    eid = lax.broadcasted_iota(jnp.int32, (N_EXPERTS, tm), 0)
    work = logits
    vals, hits = [], []
    sel = jnp.zeros((N_EXPERTS, tm), F32)
    for k in range(TOP_K):
        m = jnp.max(work, axis=0, keepdims=True)
        ik = jnp.min(jnp.where(work == m, eid, N_EXPERTS), axis=0, keepdims=True)
        hit = eid == ik
        vals.append(m)
        hits.append(hit)
        sel = jnp.where(hit, 1.0, sel)
        work = jnp.where(hit, -jnp.inf, work)
    es = [jnp.exp(v - vals[0]) for v in vals]
    den = es[0] + es[1] + es[2] + es[3]
    for k in range(TOP_K):
        gate_ref[k:k + 1, cols] = es[k] / den

    r = lax.broadcasted_iota(jnp.int32, (tm, tm), 0)
    c = lax.broadcasted_iota(jnp.int32, (tm, tm), 1)
    before = jnp.where(r < c, 1.0, 0.0).astype(BF16)
    cum = _dot(sel.astype(BF16), before)
    counts = jnp.sum(sel, axis=1, keepdims=True)
    counts = jnp.floor((counts + (RUN_ALIGN - 1)) * (1.0 / RUN_ALIGN)) * RUN_ALIGN
    er = lax.broadcasted_iota(jnp.int32, (N_EXPERTS, N_EXPERTS), 0)
    ec = lax.broadcasted_iota(jnp.int32, (N_EXPERTS, N_EXPERTS), 1)
    lower = jnp.where(er > ec, 1.0, 0.0).astype(BF16)
    offs = _dot(lower, jnp.broadcast_to(counts, (N_EXPERTS, LANES)).astype(BF16))[:, 0:1]
    pos = cum + offs
    for k in range(TOP_K):
        rk = jnp.sum(jnp.where(hits[k], pos, 0.0), axis=0, keepdims=True)
        rloc_ref[k:k + 1, cols] = rk.astype(jnp.int32)

    lane = lax.broadcasted_iota(jnp.int32, cnt_acc.shape, 1)
    cnt_acc[...] = jnp.where(lane == i, counts, cnt_acc[...])
    car_acc[...] = jnp.where(lane == i, carry_ref[:, 0:1], car_acc[...])
    carry_ref[...] = carry_ref[...] + counts


def _post_kernel(x_ref, four_ref, yf_ref, yb_ref, z_ref, gs_ref, w4_ref, wy_ref, gx_ref, wq_ref, k_ref, v_ref,
                 wo_ref, gm_ref, wr_ref, br_ref, h_ref, xn_ref, rloc_ref, gate_ref, cnt_ref, car_ref,
                 carry_ref, cnt_acc, car_acc):
    step = pl.program_id(0) * pl.num_programs(1) + pl.program_id(1)
    tm = x_ref.shape[1]

    @pl.when(step == 0)
    def _():
        carry_ref[...] = jnp.zeros_like(carry_ref)
        cnt_acc[...] = jnp.zeros_like(cnt_acc)
        car_acc[...] = jnp.zeros_like(car_acc)

    y = yf_ref[0].astype(F32) + yb_ref[0].astype(F32)
    z = z_ref[0].astype(F32)
    yn = _rms(y * (z * _sigmoid(z)), gs_ref[...]).astype(BF16)
    h = x_ref[0] + _dot(four_ref[0, 0], w4_ref[...]) + _dot(yn, wy_ref[...])

    h = _cross_attend(h, gx_ref[...], wq_ref, k_ref, v_ref, wo_ref)
    h_ref[0] = h

    xn = _rms(h, gm_ref[...])
    xn_ref[0] = xn.astype(BF16)
    w0, w1, _ = _split3(wr_ref[...])
    sub = tm // MOE_TILE
    for s in range(sub):
        rows = slice(s * MOE_TILE, (s + 1) * MOE_TILE)
        _route_tile(xn[rows], step * sub + s, rows, w0, w1, br_ref[...], rloc_ref, gate_ref,
                    carry_ref, cnt_acc, car_acc)
    cnt_ref[...] = cnt_acc[...].astype(jnp.int32)
    car_ref[...] = car_acc[...].astype(jnp.int32)


def _post(x, four, yf, yb, z, mem_k, mem_v, p, tm):
    bsz, seqlen, _ = x.shape
    t = bsz * seqlen
    nt = seqlen // tm
    n_tiles = t // MOE_TILE
    nm = mem_k.shape[1]
    weights = [p["g_ssd"], p["w4"], p["wy"], p["g_x"], p["wq"]]
    weights2 = [p["wo"], p["g_moe"], p["wr_t"], p["br"]]
    full = lambda a: pl.BlockSpec(a.shape, lambda b, i: (0,) * a.ndim)
    tok = lambda w: pl.BlockSpec((1, tm, w), lambda b, i: (b, i, 0))
    mem = pl.BlockSpec((1, nm, D_MODEL), lambda b, i: (b, 0, 0))
    kt = pl.BlockSpec((TOP_K, tm), lambda b, i: (0, b * nt + i))
    per_tile = pl.BlockSpec((N_EXPERTS, n_tiles), lambda b, i: (0, 0))
    return pl.pallas_call(
        _post_kernel,
        grid=(bsz, nt),
        in_specs=[tok(D_MODEL),
                  pl.BlockSpec((1, 1, tm, FOURIER_W), lambda b, i: (
                      b, i // (nt // 2), jnp.where(i < nt // 2, i, nt - 1 - i), 0)),
                  tok(SSD_W), tok(SSD_W), tok(SSD_W)]
                 + [full(a) for a in weights] + [mem, mem] + [full(a) for a in weights2],
        out_specs=[tok(D_MODEL), tok(D_MODEL), kt, kt, per_tile, per_tile],
        out_shape=[jax.ShapeDtypeStruct((bsz, seqlen, D_MODEL), F32),
                   jax.ShapeDtypeStruct((bsz, seqlen, D_MODEL), BF16),
                   jax.ShapeDtypeStruct((TOP_K, t), jnp.int32),
                   jax.ShapeDtypeStruct((TOP_K, t), F32),
                   jax.ShapeDtypeStruct((N_EXPERTS, n_tiles), jnp.int32),
                   jax.ShapeDtypeStruct((N_EXPERTS, n_tiles), jnp.int32)],
        scratch_shapes=[pltpu.VMEM((N_EXPERTS, LANES), F32),
                        pltpu.VMEM((N_EXPERTS, n_tiles), F32),
                        pltpu.VMEM((N_EXPERTS, n_tiles), F32)],
        compiler_params=_cparams(("arbitrary", "arbitrary")),
        name="post",
    )(x, four, yf, yb, z, *weights, mem_k, mem_v, *weights2)


def _pow2_pieces(n, max_size, piece_fn):
    done = jnp.int32(0)
    size = max_size
    while size >= RUN_ALIGN:
        bit = n & size

        @pl.when(bit != 0)
        def _(size=size, done=done):
            piece_fn(done, size)

        done = done + bit
        size //= 2


def _for_each_run(tile, cnt_ref, car_ref, pstart_ref, tile_rows, piece_fn):
    def per_expert(e, local):
        n = cnt_ref[tile * N_EXPERTS + e]
        local = pl.multiple_of(local, RUN_ALIGN)
        slot = pl.multiple_of(pstart_ref[e] + car_ref[tile * N_EXPERTS + e], RUN_ALIGN)
        _pow2_pieces(n, tile_rows, lambda done, size: piece_fn(
            pl.multiple_of(local + done, RUN_ALIGN), pl.multiple_of(slot + done, RUN_ALIGN), size))
        return local + n

    return lax.fori_loop(0, N_EXPERTS, per_expert, jnp.int32(0))


def _sorted_rows(tile_rows):
    return TOP_K * tile_rows + N_EXPERTS * RUN_ALIGN


def _sorted_onehot(rloc_ref, tile_rows):
    rows = lax.broadcasted_iota(jnp.int32, (_sorted_rows(tile_rows), tile_rows), 0)
    return [rows == rloc_ref[k:k + 1, :] for k in range(TOP_K)]


def _dispatch_kernel(pstart_ref, cnt_ref, car_ref, zstart_ref, nzero_ref, x_ref, rloc_ref, xs_hbm,
                     ybuf, zbuf, total_s, sem_z, sem_y):
    i = pl.program_id(0)
    td = x_ref.shape[0]
    slot = i % 2

    def wait_slot(s):
        def wait_piece(done, size):
            del done
            pltpu.make_async_copy(ybuf.at[s, pl.ds(0, size)], xs_hbm.at[pl.ds(0, size)],
                                  sem_y.at[s]).wait()

        _pow2_pieces(total_s[s], TOP_K * td, wait_piece)

    @pl.when(i == 0)
    def _():
        zbuf[...] = jnp.zeros_like(zbuf)

        def zcopy(start):
            dst = pl.ds(pl.multiple_of(start, MOE_BLK), MOE_BLK)
            return pltpu.make_async_copy(zbuf, xs_hbm.at[dst], sem_z)

        def issue(j, carry):
            zcopy(zstart_ref[j]).start()
            return carry

        def issue_tail(j, carry):
            zcopy(j * MOE_BLK).start()
            return carry

        def drain(j, carry):
            zcopy(0).wait()
            return carry

        n_blocks = xs_hbm.shape[0] // MOE_BLK
        lax.fori_loop(0, nzero_ref[0], issue, 0)
        lax.fori_loop(nzero_ref[1], n_blocks, issue_tail, 0)
        lax.fori_loop(0, nzero_ref[0] + (n_blocks - nzero_ref[1]), drain, 0)

    @pl.when(i >= 2)
    def _():
        wait_slot(slot)

    onehot = jnp.zeros((_sorted_rows(td), td), F32)
    for hit in _sorted_onehot(rloc_ref, td):
        onehot = jnp.where(hit, 1.0, onehot)
    ybuf[slot] = _pack_halves(_dot(onehot.astype(BF16), x_ref[...]))

    def send(local, dst, size):
        pltpu.make_async_copy(ybuf.at[slot, pl.ds(local, size)], xs_hbm.at[pl.ds(dst, size)],
                              sem_y.at[slot]).start()

    total_s[slot] = _for_each_run(i, cnt_ref, car_ref, pstart_ref, td, send)

    @pl.when(i == pl.num_programs(0) - 1)
    def _():
        wait_slot(slot)

        @pl.when(i >= 1)
        def _():
            wait_slot(1 - slot)


def _dispatch(pstart, cnt, car, zstart, nzero, xn, rloc, n_slots, td):
    t = xn.shape[0]
    half = D_MODEL // 2
    return pl.pallas_call(
        _dispatch_kernel,
        grid_spec=pltpu.PrefetchScalarGridSpec(
            num_scalar_prefetch=5,
            grid=(t // td,),
            in_specs=[pl.BlockSpec((td, D_MODEL), lambda i, *_: (i, 0)),
                      pl.BlockSpec((TOP_K, td), lambda i, *_: (0, i))],
            out_specs=pl.BlockSpec(memory_space=pl.ANY),
            scratch_shapes=[pltpu.VMEM((2, _sorted_rows(td), half), jnp.uint32),
                            pltpu.VMEM((MOE_BLK, half), jnp.uint32),
                            pltpu.SMEM((2,), jnp.int32),
                            pltpu.SemaphoreType.DMA,
                            pltpu.SemaphoreType.DMA((2,))]),
        out_shape=jax.ShapeDtypeStruct((n_slots, half), jnp.uint32),
        compiler_params=_cparams(("arbitrary",)),
        name="dispatch",
    )(pstart, cnt, car, zstart, nzero, xn, rloc)


def _expert_kernel(be_ref, na_ref, xs_ref, w1_ref, b1_ref, w2_ref, b2_ref, o_ref):
    i = pl.program_id(0)

    @pl.when(i < na_ref[0])
    def _():
        h = _dot(_unpack_halves(xs_ref[...]), w1_ref[0]) + b1_ref[0]
        glu = jnp.minimum(h[:, :D_FF], SWIGLU_LIMIT)
        lin = jnp.clip(h[:, D_FF:], -SWIGLU_LIMIT, SWIGLU_LIMIT)
        act = glu * _sigmoid(SWIGLU_ALPHA * glu) * (lin + 1.0)
        o = (_dot(act.astype(BF16), w2_ref[0]) + b2_ref[0]).astype(BF16).astype(F32)
        o_ref[...] = _pack_halves(o)

    @pl.when(i >= na_ref[0])
    def _():
        o_ref[...] = jnp.zeros_like(o_ref)


def _experts(block_e, n_active, xs, w1, b1, w2, b2):
    n_slots = xs.shape[0]
    return pl.pallas_call(
        _expert_kernel,
        grid_spec=pltpu.PrefetchScalarGridSpec(
            num_scalar_prefetch=2,
            grid=(n_slots // MOE_BLK,),
            in_specs=[pl.BlockSpec((MOE_BLK, D_MODEL // 2),
                                   lambda i, be, na: (jnp.minimum(i, na[0] - 1), 0)),
                      pl.BlockSpec((1, D_MODEL, 2 * D_FF), lambda i, be, na: (be[i], 0, 0)),
                      pl.BlockSpec((1, 1, 2 * D_FF), lambda i, be, na: (be[i], 0, 0)),
                      pl.BlockSpec((1, D_FF, D_MODEL), lambda i, be, na: (be[i], 0, 0)),
                      pl.BlockSpec((1, 1, D_MODEL), lambda i, be, na: (be[i], 0, 0))],
            out_specs=pl.BlockSpec((MOE_BLK, D_MODEL // 2), lambda i, be, na: (i, 0))),
        out_shape=jax.ShapeDtypeStruct((n_slots, D_MODEL // 2), jnp.uint32),
        compiler_params=_cparams(("arbitrary",)),
        name="experts",
    )(block_e, n_active, xs, w1, b1, w2, b2)


def _combine_kernel(pstart_ref, cnt_ref, car_ref, h_ref, rloc_ref, gate_ref, gf_ref, outs_hbm, o_ref,
                    ybuf, total_s, sem_y):
    i = pl.program_id(0)
    nt = pl.num_programs(0)
    tc = h_ref.shape[0]
    slot = i % 2

    def fetch(tile, s):
        def piece(local, src, size):
            pltpu.make_async_copy(outs_hbm.at[pl.ds(src, size)], ybuf.at[s, pl.ds(local, size)],
                                  sem_y.at[s]).start()

        total_s[s] = _for_each_run(tile, cnt_ref, car_ref, pstart_ref, tc, piece)

    @pl.when(i == 0)
    def _():
        fetch(i, slot)

    @pl.when(i + 1 < nt)
    def _():
        fetch(i + 1, 1 - slot)

    hits = _sorted_onehot(rloc_ref, tc)
    gsel = jnp.zeros((_sorted_rows(tc), tc), F32)
    for k in range(TOP_K):
        gsel = jnp.where(hits[k], gate_ref[k:k + 1, :], gsel)
    g_t = gsel.T
    g0 = g_t.astype(BF16)
    g1 = (g_t - g0.astype(F32)).astype(BF16)

    def wait_piece(done, size):
        del done
        pltpu.make_async_copy(outs_hbm.at[pl.ds(0, size)], ybuf.at[slot, pl.ds(0, size)],
                              sem_y.at[slot]).wait()

    _pow2_pieces(total_s[slot], TOP_K * tc, wait_piece)
    row = lax.broadcasted_iota(jnp.int32, (_sorted_rows(tc), 1), 0)
    yb = _unpack_halves(jnp.where(row < total_s[slot], ybuf[slot], jnp.uint32(0)))
    o_ref[...] = _rms(h_ref[...] + _dot(g0, yb) + _dot(g1, yb), gf_ref[...])


def _combine(pstart, cnt, car, h2d, rloc, gate, gf, outs, tc):
    t = h2d.shape[0]
    kt = pl.BlockSpec((TOP_K, tc), lambda i, *_: (0, i))
    return pl.pallas_call(
        _combine_kernel,
        grid_spec=pltpu.PrefetchScalarGridSpec(
            num_scalar_prefetch=3,
            grid=(t // tc,),
            in_specs=[pl.BlockSpec((tc, D_MODEL), lambda i, *_: (i, 0)), kt, kt,
                      pl.BlockSpec((1, D_MODEL), lambda i, *_: (0, 0)),
                      pl.BlockSpec(memory_space=pl.ANY)],
            out_specs=pl.BlockSpec((tc, D_MODEL), lambda i, *_: (i, 0)),
            scratch_shapes=[pltpu.VMEM((2, _sorted_rows(tc), D_MODEL // 2), jnp.uint32),
                            pltpu.SMEM((2,), jnp.int32),
                            pltpu.SemaphoreType.DMA((2,))]),
        out_shape=jax.ShapeDtypeStruct((t, D_MODEL), F32),
        compiler_params=_cparams(("arbitrary",)),
        name="combine",
    )(pstart, cnt, car, h2d, rloc, gate, gf, outs)


def _prep_weights(g_mix_norm, w_in, conv_w, conv_b, a_log_fwd, a_log_bwd, dt_bias_fwd, dt_bias_bwd,
                  d_skip, g_fourier_norm, g_ssd_norm, w_out, g_xattn_norm, g_mem_norm, w_q, w_k,
                  w_v, w_o, g_moe_norm, w_router, b_router, w_mlp1, b_mlp1, w_mlp2, b_mlp2, g_final):
    s1 = FOURIER_W
    s2 = s1 + SSD_W
    s3 = s2 + CONV_CH
    w = w_in[0]
    wdt = jnp.zeros((D_MODEL, 2 * LANES), F32)
    wdt = wdt.at[:, :SSD_HEADS].set(w[:, s3:s3 + SSD_HEADS])
    wdt = wdt.at[:, LANES:LANES + SSD_HEADS].set(w[:, s3 + SSD_HEADS:])
    pad_heads = lambda a, b: jnp.zeros((2, 1, LANES), F32).at[0, 0, :SSD_HEADS].set(a[0]).at[
        1, 0, :SSD_HEADS].set(b[0])
    head_major = lambda a: jnp.broadcast_to(jnp.swapaxes(a, 1, 2), (2, LANES, LANES))
    alog = pad_heads(a_log_fwd, a_log_bwd)
    dtb = pad_heads(dt_bias_fwd, dt_bias_bwd)
    dsk = jnp.zeros((2, 1, SSD_W), F32).at[0, 0].set(jnp.repeat(d_skip[0], SSD_HEAD_DIM))
    lower = np.tril(np.ones((CHUNK, CHUNK), np.float32))
    return dict(
        g_mix=g_mix_norm[0][None], wu=w[:, :s1].astype(BF16), wz=w[:, s1:s2].astype(BF16),
        wx=w[:, s2:s3].astype(BF16), wdt=wdt.astype(BF16), wdtt=wdt.T.astype(BF16),
        cs=_channel_dft_table(), conv_w=conv_w[0, :, 0, :], conv_b=conv_b[0][None],
        tri=jnp.asarray(np.stack([lower, lower.T])), alog=alog, alogt=head_major(alog),
        dtb=dtb, dtbt=head_major(dtb), dsk=dsk,
        g_four=g_fourier_norm[0][None], g_ssd=g_ssd_norm[0][None],
        w4=w_out[0, :FOURIER_W].astype(BF16), wy=w_out[0, FOURIER_W:].astype(BF16),
        g_x=g_xattn_norm[0][None], g_mem=g_mem_norm[0][None],
        wq=w_q[0].astype(BF16), wk=w_k[0].astype(BF16), wv=w_v[0].astype(BF16), wo=w_o[0].astype(BF16),
        g_moe=g_moe_norm[0][None], wr_t=w_router[0].T, br=b_router[0][:, None],
        w1=w_mlp1[0].astype(BF16), b1=b_mlp1[0][:, None, :], w2=w_mlp2[0].astype(BF16),
        b2=b_mlp2[0][:, None, :], g_final=g_final[None])


def _encode_group(x, mem, p):
    bsz, seqlen, _ = x.shape
    t = bsz * seqlen
    tm = min(TOKEN_TILE, seqlen // 2)
    assert seqlen % (2 * tm) == 0 and tm % MOE_TILE == 0 and seqlen % CHUNK == 0, x.shape

    ab, z, xbc, dt, dtt = _inproj(x, p["g_mix"], p["wu"], p["wz"], p["wx"], p["wdt"], p["wdtt"],
                                  p["cs"], p["conv_w"], p["conv_b"], tm)
    cos_t, nsin_t, cmid_t = _dft_tables(seqlen)
    nb = 4 if bsz % 4 == 0 else (2 if bsz % 2 == 0 else 1)
    four = _fourier(ab.reshape(bsz, seqlen, 2 * FOURIER_W), cos_t, nsin_t, cmid_t, p["g_four"], nb,
                    tm, min(DFT_TILE, seqlen))
    yf, yb = _ssd(xbc.reshape(bsz, seqlen, CONV_CH), dt.reshape(bsz, seqlen, 2 * LANES), dtt,
                  p["tri"], p["alog"], p["alogt"], p["dtb"], p["dtbt"], p["dsk"])

    k, v = _kv(mem, p["g_mem"], p["wk"], p["wv"])
    h, xn, rloc, gate, cnt, car = _post(x, four, yf, yb, z.reshape(bsz, seqlen, SSD_W), k, v, p, tm)
    h = h.reshape(t, D_MODEL)
    xn = xn.reshape(t, D_MODEL)

    td = MOE_TILE
    counts = jnp.sum(cnt, axis=1)
    padded = ((counts + MOE_BLK - 1) // MOE_BLK) * MOE_BLK
    pend = jnp.cumsum(padded)
    pstart = (pend - padded).astype(jnp.int32)
    max_rows = t * TOP_K + (t // td) * N_EXPERTS * (RUN_ALIGN - 1)
    n_blocks = -(-max_rows // MOE_BLK) + N_EXPERTS
    n_slots = n_blocks * MOE_BLK
    n_active = (pend[-1] // MOE_BLK).astype(jnp.int32)
    blk = jnp.arange(n_blocks, dtype=jnp.int32)
    be = jnp.sum(pend[None, :] <= (jnp.minimum(blk, n_active - 1) * MOE_BLK)[:, None], axis=1)
    be = jnp.minimum(be, N_EXPERTS - 1).astype(jnp.int32)
    zstart = (pend - MOE_BLK).astype(jnp.int32)[jnp.argsort(padded == 0)]
    nzero = jnp.stack([jnp.sum(padded > 0).astype(jnp.int32), n_active])

    cnt_flat = cnt.T.reshape(-1)
    car_flat = car.T.reshape(-1)
    xs = _dispatch(pstart, cnt_flat, car_flat, zstart, nzero, xn, rloc, n_slots, td)
    outs = _experts(be, n_active[None], xs, p["w1"], p["b1"], p["w2"], p["b2"])
    y = _combine(pstart, cnt_flat, car_flat, h, rloc, gate, p["g_final"], outs, td)
    return y.reshape(bsz, seqlen, D_MODEL)


def kernel(x_prompt, x_sample, mem_prompt, mem_sample, g_mix_norm, w_in, conv_w, conv_b, a_log_fwd, a_log_bwd, dt_bias_fwd, dt_bias_bwd, d_skip, g_fourier_norm, g_ssd_norm, w_out, g_xattn_norm, g_mem_norm, w_q, w_k, w_v, w_o, g_moe_norm, w_router, b_router, w_mlp1, b_mlp1, w_mlp2, b_mlp2, g_final):
    p = _prep_weights(g_mix_norm, w_in, conv_w, conv_b, a_log_fwd, a_log_bwd, dt_bias_fwd,
                      dt_bias_bwd, d_skip, g_fourier_norm, g_ssd_norm, w_out, g_xattn_norm,
                      g_mem_norm, w_q, w_k, w_v, w_o, g_moe_norm, w_router, b_router, w_mlp1,
                      b_mlp1, w_mlp2, b_mlp2, g_final)
    return (_encode_group(x_prompt, mem_prompt, p), _encode_group(x_sample, mem_sample, p))
```

```python
import functools

import numpy as np
import jax
import jax.numpy as jnp
from jax import lax
from jax.experimental import pallas as pl
from jax.experimental.pallas import tpu as pltpu

F32 = jnp.float32
BF16 = jnp.bfloat16

D_MODEL = 1024
D_MIX = 2 * D_MODEL
FOURIER_W = D_MIX // 4
FOURIER_GROUPS = 8
FOURIER_GROUP_DIM = FOURIER_W // FOURIER_GROUPS
SSD_W = D_MIX - FOURIER_W
SSD_HEAD_DIM = 64
SSD_HEADS = SSD_W // SSD_HEAD_DIM
SSD_GROUPS = 4
SSD_HPG = SSD_HEADS // SSD_GROUPS
D_STATE = 128
CONV_WIDTH = 5
CONV_PAD = CONV_WIDTH // 2
GN = SSD_GROUPS * D_STATE
CONV_CH = SSD_W + 2 * GN
CHUNK = 128
XATTN_HEADS = 4
XATTN_HEAD_DIM = D_MODEL // XATTN_HEADS
N_EXPERTS = 32
TOP_K = 4
D_FF = D_MODEL
SWIGLU_ALPHA = 1.702
SWIGLU_LIMIT = 7.0
NORM_EPS = 1e-5
LOG2E = 1.4426950408889634

LANES = 128
HALO = 8
CONV_COLS = 512
TOKEN_TILE = 512
DFT_TILE = 1024
CARRY_ROWS = 16
MOE_BLK = 512
MOE_TILE = 256
RUN_ALIGN = 8
RARE_RUN = 64
VMEM_LIMIT = 48 * 1024 * 1024


def _cparams(sem):
    return pltpu.CompilerParams(dimension_semantics=sem, vmem_limit_bytes=VMEM_LIMIT)


def _rms(x, g):
    return x * lax.rsqrt(jnp.mean(x * x, axis=-1, keepdims=True) + NORM_EPS) * g


def _sigmoid(x):
    return 1.0 / (1.0 + jnp.exp(-x))


def _softplus(x):
    return jnp.maximum(x, 0.0) + jnp.log1p(jnp.exp(-jnp.abs(x)))


def _dot(a, b):
    return jnp.dot(a, b, preferred_element_type=F32)


def _dot_nt(a, b):
    return lax.dot_general(a, b, (((1,), (1,)), ((), ())), preferred_element_type=F32)


def _pack_halves(x):
    c = x.shape[1] // 2
    u = lax.bitcast_convert_type(x, jnp.uint32)
    return (u[:, :c] >> 16) | (u[:, c:] & jnp.uint32(0xFFFF0000))


def _unpack_halves(w):
    lo = lax.bitcast_convert_type(w << 16, F32)
    hi = lax.bitcast_convert_type(w & jnp.uint32(0xFFFF0000), F32)
    return jnp.concatenate([lo, hi], axis=1).astype(BF16)


def _split3(x):
    h = x.astype(BF16)
    r = x - h.astype(F32)
    m = r.astype(BF16)
    l = (r - m.astype(F32)).astype(BF16)
    return h, m, l


def _inproj_kernel(prev_ref, x_ref, next_ref, g_ref, wu_ref, wz_ref, wx_ref, wdt_ref, wdtt_ref,
                   cs_ref, cw_ref, cb_ref, ab_ref, z_ref, xbc_ref, dt_ref, dtt_ref):
    i = pl.program_id(1)
    tm = x_ref.shape[1]
    rows = tm + 2 * HALO
    mid = slice(HALO, HALO + tm)
    xa = jnp.concatenate([prev_ref[0], x_ref[0], next_ref[0]], axis=0)
    xn = _rms(xa, g_ref[...]).astype(BF16)
    u = _dot(xn, wu_ref[...])[mid]
    ab_ref[...] = _dot(u.astype(BF16), cs_ref[...]).astype(BF16)
    z_ref[...] = _dot(xn, wz_ref[...])[mid].astype(BF16)
    dt_ref[...] = _dot(xn, wdt_ref[...])[mid]
    dtt_ref[...] = _dot_nt(wdtt_ref[...], xn)[:, mid]

    r = lax.broadcasted_iota(jnp.int32, (rows, 1), 0)
    has_prev = jnp.where(i > 0, 1.0, 0.0)
    has_next = jnp.where(i < pl.num_programs(1) - 1, 1.0, 0.0)
    inside = jnp.where(r < HALO, has_prev, jnp.where(r >= HALO + tm, has_next, 1.0))
    for c in range(CONV_CH // CONV_COLS):
        cols = slice(c * CONV_COLS, (c + 1) * CONV_COLS)
        xe = _dot(xn, wx_ref[:, cols]) * inside
        acc = cb_ref[:, cols] + cw_ref[CONV_PAD:CONV_PAD + 1, cols] * xe
        for w in range(CONV_WIDTH):
            if w != CONV_PAD:
                acc = acc + cw_ref[w:w + 1, cols] * pltpu.roll(xe, (CONV_PAD - w) % rows, axis=0)
        y = acc[mid]
        xbc_ref[:, cols] = (y * _sigmoid(y)).astype(BF16)


def _inproj(x, g, wu, wz, wx, wdt, wdtt, cs, cw, cb, tm):
    bsz, seqlen, _ = x.shape
    t = bsz * seqlen
    nt = seqlen // tm
    r = tm // HALO
    nh = seqlen // HALO
    full = lambda a: pl.BlockSpec(a.shape, lambda b, i: (0,) * a.ndim)
    row = lambda w: pl.BlockSpec((tm, w), lambda b, i: (b * nt + i, 0))
    return pl.pallas_call(
        _inproj_kernel,
        grid=(bsz, nt),
        in_specs=[pl.BlockSpec((1, HALO, D_MODEL), lambda b, i: (b, jnp.maximum(i * r - 1, 0), 0)),
                  pl.BlockSpec((1, tm, D_MODEL), lambda b, i: (b, i, 0)),
                  pl.BlockSpec((1, HALO, D_MODEL), lambda b, i: (b, jnp.minimum((i + 1) * r, nh - 1), 0)),
                  full(g), full(wu), full(wz), full(wx), full(wdt), full(wdtt), full(cs), full(cw),
                  full(cb)],
        out_specs=[row(2 * FOURIER_W), row(SSD_W), row(CONV_CH), row(2 * LANES),
                   pl.BlockSpec((2 * LANES, tm), lambda b, i: (0, b * nt + i))],
        out_shape=[jax.ShapeDtypeStruct((t, 2 * FOURIER_W), BF16),
                   jax.ShapeDtypeStruct((t, SSD_W), BF16),
                   jax.ShapeDtypeStruct((t, CONV_CH), BF16),
                   jax.ShapeDtypeStruct((t, 2 * LANES), F32),
                   jax.ShapeDtypeStruct((2 * LANES, t), F32)],
        compiler_params=_cparams(("parallel", "parallel")),
        name="inproj",
    )(x, x, x, g, wu, wz, wx, wdt, wdtt, cs, cw, cb)


def _fourier_kernel(c_ref, s_ref, cm_ref, ab_ref, g_ref, o_ref, accp_ref, accq_ref, accm_ref,
                    carry_ref, *, scale, nb):
    j = pl.program_id(1)
    k = pl.program_id(2)
    tr = c_ref.shape[0]

    @pl.when(k == 0)
    def _():
        accp_ref[...] = jnp.zeros_like(accp_ref)
        accq_ref[...] = jnp.zeros_like(accq_ref)

    @pl.when((k == 0) & (j == 0))
    def _():
        accm_ref[...] = jnp.zeros_like(accm_ref)

    c = c_ref[...]
    s = s_ref[...]
    for b in range(nb):
        accp_ref[b] += _dot(c, ab_ref[b, :, :FOURIER_W])
        accq_ref[b] += _dot(s, ab_ref[b, :, FOURIER_W:])

    @pl.when(j == 0)
    def _():
        for b in range(nb):
            accm_ref[b] += _dot(cm_ref[...], ab_ref[b, :, :FOURIER_W])

    @pl.when(k == pl.num_programs(2) - 1)
    def _():
        g = g_ref[...]
        r = lax.broadcasted_iota(jnp.int32, (tr, tr + CARRY_ROWS), 0)
        col = lax.broadcasted_iota(jnp.int32, (tr, tr + CARRY_ROWS), 1)
        flip = jnp.where(col == jnp.where(r == 0, tr, tr - r), 1.0, 0.0).astype(BF16)
        for b in range(nb):
            @pl.when(j == 0)
            def _():
                carry_ref[b] = _rms(accm_ref[b] * scale, g).astype(BF16)

            p = accp_ref[b] * scale
            q = accq_ref[b] * scale
            o_ref[b, 0] = _rms(p + q, g).astype(BF16)
            m = _rms(p - q, g).astype(BF16)
            o_ref[b, 1] = _dot(flip, jnp.concatenate([m, carry_ref[b]], axis=0)).astype(BF16)
            carry_ref[b] = m[:CARRY_ROWS]


def _fourier(ab, cos_t, nsin_t, cmid_t, g, nb, tr, tk):
    bsz, seqlen, _ = ab.shape
    half_tiles = seqlen // (2 * tr)
    scale = float(1.0 / np.sqrt(seqlen * FOURIER_GROUP_DIM))
    tile = lambda b, j, k: (half_tiles - 1 - j, k)
    return pl.pallas_call(
        functools.partial(_fourier_kernel, scale=scale, nb=nb),
        grid=(bsz // nb, half_tiles, seqlen // tk),
        in_specs=[pl.BlockSpec((tr, tk), tile),
                  pl.BlockSpec((tr, tk), tile),
                  pl.BlockSpec((CARRY_ROWS, tk), lambda b, j, k: (0, k)),
                  pl.BlockSpec((nb, tk, 2 * FOURIER_W), lambda b, j, k: (b, k, 0)),
                  pl.BlockSpec((1, FOURIER_W), lambda b, j, k: (0, 0))],
        out_specs=pl.BlockSpec((nb, 2, tr, FOURIER_W), lambda b, j, k: (b, 0, half_tiles - 1 - j, 0)),
        out_shape=jax.ShapeDtypeStruct((bsz, 2, seqlen // 2, FOURIER_W), BF16),
        scratch_shapes=[pltpu.VMEM((nb, tr, FOURIER_W), F32),
                        pltpu.VMEM((nb, tr, FOURIER_W), F32),
                        pltpu.VMEM((nb, CARRY_ROWS, FOURIER_W), F32),
                        pltpu.VMEM((nb, CARRY_ROWS, FOURIER_W), BF16)],
        compiler_params=_cparams(("parallel", "arbitrary", "arbitrary")),
        name="fourier",
    )(cos_t, nsin_t, cmid_t, ab, g)


def _dft_tables(seqlen):
    lo = min(seqlen // 2, LANES)
    hi = seqlen // 2 // lo
    l = np.arange(seqlen, dtype=np.int64)
    a_hi = 2.0 * np.pi * ((np.arange(hi, dtype=np.int64)[:, None] * lo * l[None, :]) % seqlen) / seqlen
    a_lo = 2.0 * np.pi * ((np.arange(lo, dtype=np.int64)[:, None] * l[None, :]) % seqlen) / seqlen
    c1 = jnp.asarray(np.cos(a_hi), F32)[:, None, :]
    s1 = jnp.asarray(np.sin(a_hi), F32)[:, None, :]
    c0 = jnp.asarray(np.cos(a_lo), F32)[None, :, :]
    s0 = jnp.asarray(np.sin(a_lo), F32)[None, :, :]
    cos_t = (c1 * c0 - s1 * s0).reshape(seqlen // 2, seqlen).astype(BF16)
    nsin_t = (-(s1 * c0 + c1 * s0)).reshape(seqlen // 2, seqlen).astype(BF16)
    cmid = np.zeros((CARRY_ROWS, seqlen), np.float32)
    cmid[0] = 1.0 - 2.0 * (l % 2)
    return cos_t, nsin_t, jnp.asarray(cmid, BF16)


def _channel_dft_table():
    c = np.arange(FOURIER_W)
    same = (c[:, None] // FOURIER_GROUP_DIM) == (c[None, :] // FOURIER_GROUP_DIM)
    ang = 2.0 * np.pi * ((c[:, None] % FOURIER_GROUP_DIM) * (c[None, :] % FOURIER_GROUP_DIM)
                         % FOURIER_GROUP_DIM) / FOURIER_GROUP_DIM
    cc = np.where(same, np.cos(ang), 0.0)
    sc = np.where(same, np.sin(ang), 0.0)
    return jnp.asarray(np.concatenate([cc, sc], axis=1), BF16)


def _ssd_kernel(xf_ref, xb_ref, dtf_ref, dtb_ref, dttf_ref, dttb_ref, tri_ref, alog_ref, alogt_ref,
                bias_ref, biast_ref, dsk_ref, yf_ref, yb_ref, state_ref):
    @pl.when(pl.program_id(1) == 0)
    def _():
        state_ref[...] = jnp.zeros_like(state_ref)

    for d, (x_ref, dt_ref, dtt_ref, y_ref) in enumerate(((xf_ref, dtf_ref, dttf_ref, yf_ref),
                                                         (xb_ref, dtb_ref, dttb_ref, yb_ref))):
        _ssd_chunk(x_ref, dt_ref, dtt_ref, tri_ref.at[d], tri_ref.at[1 - d], alog_ref.at[d],
                   alogt_ref.at[d], bias_ref.at[d], biast_ref.at[d], dsk_ref.at[d], y_ref,
                   state_ref.at[d])


def _ssd_chunk(xbc_ref, dt_ref, dtt_ref, tri_ref, trit_ref, alog_ref, alogt_ref, dtb_ref, dtbt_ref,
               dsk_ref, y_ref, state_ref):
    q = CHUNK
    tri = tri_ref[...]
    mask = tri > 0.5
    dt = _softplus(dt_ref[0] + dtb_ref[...])
    d0, d1, d2 = _split3(dt * -jnp.exp(alog_ref[...]))
    tri_b = tri.astype(BF16)
    acum2 = (_dot(tri_b, d0) + _dot(tri_b, d1) + _dot(tri_b, d2)) * LOG2E
    dt_t = _softplus(dtt_ref[...] + dtbt_ref[...])
    t0, t1, t2 = _split3(dt_t * -jnp.exp(alogt_ref[...]))
    trit_b = trit_ref[...].astype(BF16)
    ones_b = jnp.ones((q, LANES), BF16)
    acum_t2 = (_dot(t0, trit_b) + _dot(t1, trit_b) + _dot(t2, trit_b)) * LOG2E
    tot2 = (_dot(t0, ones_b) + _dot(t1, ones_b) + _dot(t2, ones_b)) * LOG2E
    row2 = acum_t2 - jnp.log2(dt_t)
    w_t = dt_t * jnp.exp2(tot2 - acum_t2)
    etot = jnp.exp2(tot2)
    lane_lo = lax.broadcasted_iota(jnp.int32, (1, LANES), 1) < SSD_HEAD_DIM

    for g in range(SSD_GROUPS):
        bm = xbc_ref[0, :, SSD_W + g * D_STATE:SSD_W + (g + 1) * D_STATE]
        cm = xbc_ref[0, :, SSD_W + GN + g * D_STATE:SSD_W + GN + (g + 1) * D_STATE]
        cb = _dot_nt(cm, bm)
        cm_f = cm.astype(F32)
        bm_t = bm.astype(F32).T
        for p in range(SSD_HPG // 2):
            t = g * (SSD_HPG // 2) + p
            cols = slice(t * LANES, (t + 1) * LANES)
            xs = xbc_ref[0, :, cols]
            st = state_ref[:, cols]
            lhs = []
            lhs2 = []
            for h in (2 * t, 2 * t + 1):
                colb2 = jnp.broadcast_to(acum2[:, h:h + 1], (q, q))
                decay_dt = jnp.where(mask, jnp.exp2(colb2 - row2[h:h + 1, :]), 0.0)
                lhs.append(jnp.concatenate([(cb * decay_dt).astype(BF16),
                                            (cm_f * jnp.exp2(colb2)).astype(BF16)], axis=1))
                lhs2.append((bm_t * w_t[h:h + 1, :]).astype(BF16))
            yy = _dot(jnp.concatenate(lhs, axis=0), jnp.concatenate([xs, st.astype(BF16)], axis=0))
            y = jnp.where(lane_lo, yy[:q], yy[q:]) + dsk_ref[:, cols] * xs.astype(F32)
            y_ref[0, :, cols] = y.astype(BF16)
            dec = jnp.where(lane_lo, etot[2 * t:2 * t + 1, :], etot[2 * t + 1:2 * t + 2, :])
            ss = _dot(jnp.concatenate(lhs2, axis=0), xs)
            state_ref[:, cols] = st * dec + jnp.where(lane_lo, ss[:D_STATE], ss[D_STATE:])


def _ssd(xbc, dt, dtt, tri, alog, alogt, dtb, dtbt, dsk):
    bsz, seqlen, _ = xbc.shape
    nc = seqlen // CHUNK
    full = lambda a: pl.BlockSpec(a.shape, lambda b, j: (0,) * a.ndim)
    return pl.pallas_call(
        _ssd_kernel,
        grid=(bsz, nc),
        in_specs=[pl.BlockSpec((1, CHUNK, CONV_CH), lambda b, j: (b, j, 0)),
                  pl.BlockSpec((1, CHUNK, CONV_CH), lambda b, j: (b, nc - 1 - j, 0)),
                  pl.BlockSpec((1, CHUNK, LANES), lambda b, j: (b, j, 0)),
                  pl.BlockSpec((1, CHUNK, LANES), lambda b, j: (b, nc - 1 - j, 1)),
                  pl.BlockSpec((LANES, CHUNK), lambda b, j: (0, b * nc + j)),
                  pl.BlockSpec((LANES, CHUNK), lambda b, j: (1, b * nc + nc - 1 - j)),
                  full(tri), full(alog), full(alogt), full(dtb), full(dtbt), full(dsk)],
        out_specs=[pl.BlockSpec((1, CHUNK, SSD_W), lambda b, j: (b, j, 0)),
                   pl.BlockSpec((1, CHUNK, SSD_W), lambda b, j: (b, nc - 1 - j, 0))],
        out_shape=[jax.ShapeDtypeStruct((bsz, seqlen, SSD_W), BF16)] * 2,
        scratch_shapes=[pltpu.VMEM((2, D_STATE, SSD_W), F32)],
        compiler_params=_cparams(("parallel", "arbitrary")),
        name="ssd",
    )(xbc, xbc, dt, dt, dtt, dtt, tri, alog, alogt, dtb, dtbt, dsk)


def _kv_kernel(mem_ref, g_ref, wk_ref, wv_ref, k_ref, v_ref):
    mn = _rms(mem_ref[0], g_ref[...]).astype(BF16)
    k_ref[0] = _dot(mn, wk_ref[...]).astype(BF16)
    v_ref[0] = _dot(mn, wv_ref[...]).astype(BF16)


def _kv(mem, g, wk, wv):
    bsz, nm, _ = mem.shape
    full = lambda a: pl.BlockSpec(a.shape, lambda b: (0,) * a.ndim)
    blk = pl.BlockSpec((1, nm, D_MODEL), lambda b: (b, 0, 0))
    return pl.pallas_call(
        _kv_kernel,
        grid=(bsz,),
        in_specs=[blk, full(g), full(wk), full(wv)],
        out_specs=[blk, blk],
        out_shape=[jax.ShapeDtypeStruct(mem.shape, BF16)] * 2,
        compiler_params=_cparams(("parallel",)),
        name="kv",
    )(mem, g, wk, wv)


def _cross_attend(h, g, wq_ref, k_ref, v_ref, wo_ref):
    hn = _rms(h, g).astype(BF16)
    q = (_dot(hn, wq_ref[...]) * (XATTN_HEAD_DIM ** -0.5)).astype(BF16)
    outs = []
    for hd in range(XATTN_HEADS):
        cols = slice(hd * XATTN_HEAD_DIM, (hd + 1) * XATTN_HEAD_DIM)
        s = _dot_nt(q[:, cols], k_ref[0, :, cols])
        e = jnp.exp(s - jnp.max(s, axis=-1, keepdims=True))
        p = (e / jnp.sum(e, axis=-1, keepdims=True)).astype(BF16)
        outs.append(_dot(p, v_ref[0, :, cols]).astype(BF16))
    return h + _dot(jnp.concatenate(outs, axis=1), wo_ref[...])


def _route_tile(xn, i, cols, w0, w1, br, rloc_ref, gate_ref, carry_ref, cnt_acc, car_acc):
    tm = xn.shape[0]
    x0, x1, _ = _split3(xn)
    logits = _dot_nt(w0, x0) + _dot_nt(w0, x1) + _dot_nt(w1, x0) + br

    eid = lax.broadcasted_iota(jnp.int32, (N_EXPERTS, tm), 0)
    work = logits
    vals, hits = [], []
    sel = jnp.zeros((N_EXPERTS, tm), F32)
    for k in range(TOP_K):
        m = jnp.max(work, axis=0, keepdims=True)
        ik = jnp.min(jnp.where(work == m, eid, N_EXPERTS), axis=0, keepdims=True)
        hit = eid == ik
        vals.append(m)
        hits.append(hit)
        sel = jnp.where(hit, 1.0, sel)
        work = jnp.where(hit, -jnp.inf, work)
    es = [jnp.exp(v - vals[0]) for v in vals]
    den = es[0] + es[1] + es[2] + es[3]
    for k in range(TOP_K):
        gate_ref[k:k + 1, cols] = es[k] / den

    r = lax.broadcasted_iota(jnp.int32, (tm, tm), 0)
    c = lax.broadcasted_iota(jnp.int32, (tm, tm), 1)
    before = jnp.where(r < c, 1.0, 0.0).astype(BF16)
    cum = _dot(sel.astype(BF16), before)
    counts = jnp.sum(sel, axis=1, keepdims=True)
    counts = jnp.floor((counts + (RUN_ALIGN - 1)) * (1.0 / RUN_ALIGN)) * RUN_ALIGN
    er = lax.broadcasted_iota(jnp.int32, (N_EXPERTS, N_EXPERTS), 0)
    ec = lax.broadcasted_iota(jnp.int32, (N_EXPERTS, N_EXPERTS), 1)
    lower = jnp.where(er > ec, 1.0, 0.0).astype(BF16)
    offs = _dot(lower, jnp.broadcast_to(counts, (N_EXPERTS, LANES)).astype(BF16))[:, 0:1]
    pos = cum + offs
    for k in range(TOP_K):
        rk = jnp.sum(jnp.where(hits[k], pos, 0.0), axis=0, keepdims=True)
        rloc_ref[k:k + 1, cols] = rk.astype(jnp.int32)

    lane = lax.broadcasted_iota(jnp.int32, cnt_acc.shape, 1)
    cnt_acc[...] = jnp.where(lane == i, counts, cnt_acc[...])
    car_acc[...] = jnp.where(lane == i, carry_ref[:, 0:1], car_acc[...])
    carry_ref[...] = carry_ref[...] + counts


def _post_kernel(x_ref, four_ref, yf_ref, yb_ref, z_ref, gs_ref, w4_ref, wy_ref, gx_ref, wq_ref, k_ref, v_ref,
                 wo_ref, gm_ref, wr_ref, br_ref, h_ref, xn_ref, rloc_ref, gate_ref, cnt_ref, car_ref,
                 carry_ref, cnt_acc, car_acc):
    step = pl.program_id(0) * pl.num_programs(1) + pl.program_id(1)
    tm = x_ref.shape[1]

    @pl.when(step == 0)
    def _():
        carry_ref[...] = jnp.zeros_like(carry_ref)
        cnt_acc[...] = jnp.zeros_like(cnt_acc)
        car_acc[...] = jnp.zeros_like(car_acc)

    y = yf_ref[0].astype(F32) + yb_ref[0].astype(F32)
    z = z_ref[0].astype(F32)
    yn = _rms(y * (z * _sigmoid(z)), gs_ref[...]).astype(BF16)
    h = x_ref[0] + _dot(four_ref[0, 0], w4_ref[...]) + _dot(yn, wy_ref[...])

    h = _cross_attend(h, gx_ref[...], wq_ref, k_ref, v_ref, wo_ref)
    h_ref[0] = h

    xn = _rms(h, gm_ref[...])
    xn_ref[0] = xn.astype(BF16)
    w0, w1, _ = _split3(wr_ref[...])
    sub = tm // MOE_TILE
    for s in range(sub):
        rows = slice(s * MOE_TILE, (s + 1) * MOE_TILE)
        _route_tile(xn[rows], step * sub + s, rows, w0, w1, br_ref[...], rloc_ref, gate_ref,
                    carry_ref, cnt_acc, car_acc)
    cnt_ref[...] = cnt_acc[...].astype(jnp.int32)
    car_ref[...] = car_acc[...].astype(jnp.int32)


def _post(x, four, yf, yb, z, mem_k, mem_v, p, tm):
    bsz, seqlen, _ = x.shape
    t = bsz * seqlen
    nt = seqlen // tm
    n_tiles = t // MOE_TILE
    nm = mem_k.shape[1]
    weights = [p["g_ssd"], p["w4"], p["wy"], p["g_x"], p["wq"]]
    weights2 = [p["wo"], p["g_moe"], p["wr_t"], p["br"]]
    full = lambda a: pl.BlockSpec(a.shape, lambda b, i: (0,) * a.ndim)
    tok = lambda w: pl.BlockSpec((1, tm, w), lambda b, i: (b, i, 0))
    mem = pl.BlockSpec((1, nm, D_MODEL), lambda b, i: (b, 0, 0))
    kt = pl.BlockSpec((TOP_K, tm), lambda b, i: (0, b * nt + i))
    per_tile = pl.BlockSpec((N_EXPERTS, n_tiles), lambda b, i: (0, 0))
    return pl.pallas_call(
        _post_kernel,
        grid=(bsz, nt),
        in_specs=[tok(D_MODEL),
                  pl.BlockSpec((1, 1, tm, FOURIER_W), lambda b, i: (
                      b, i // (nt // 2), jnp.where(i < nt // 2, i, nt - 1 - i), 0)),
                  tok(SSD_W), tok(SSD_W), tok(SSD_W)]
                 + [full(a) for a in weights] + [mem, mem] + [full(a) for a in weights2],
        out_specs=[tok(D_MODEL), tok(D_MODEL), kt, kt, per_tile, per_tile],
        out_shape=[jax.ShapeDtypeStruct((bsz, seqlen, D_MODEL), F32),
                   jax.ShapeDtypeStruct((bsz, seqlen, D_MODEL), BF16),
                   jax.ShapeDtypeStruct((TOP_K, t), jnp.int32),
                   jax.ShapeDtypeStruct((TOP_K, t), F32),
                   jax.ShapeDtypeStruct((N_EXPERTS, n_tiles), jnp.int32),
                   jax.ShapeDtypeStruct((N_EXPERTS, n_tiles), jnp.int32)],
        scratch_shapes=[pltpu.VMEM((N_EXPERTS, LANES), F32),
                        pltpu.VMEM((N_EXPERTS, n_tiles), F32),
                        pltpu.VMEM((N_EXPERTS, n_tiles), F32)],
        compiler_params=_cparams(("arbitrary", "arbitrary")),
        name="post",
    )(x, four, yf, yb, z, *weights, mem_k, mem_v, *weights2)


def _pow2_pieces(n, max_size, piece_fn, rare_size=None):
    def pieces(sizes, done):
        for size in sizes:
            bit = n & size

            @pl.when(bit != 0)
            def _(size=size, done=done):
                piece_fn(done, size)

            done = done + bit

    sizes = []
    size = max_size
    while size >= RUN_ALIGN:
        sizes.append(size)
        size //= 2
    rare = [s for s in sizes if rare_size is not None and s >= rare_size]
    if rare:
        pl.when(n >= rare_size)(lambda: pieces(rare, jnp.int32(0)))
    pieces(sizes[len(rare):], n & -rare[-1] if rare else jnp.int32(0))


def _for_each_run(tile, cnt_ref, car_ref, pstart_ref, tile_rows, piece_fn):
    def per_expert(e, local):
        n = cnt_ref[tile * N_EXPERTS + e]
        local = pl.multiple_of(local, RUN_ALIGN)
        slot = pl.multiple_of(pstart_ref[e] + car_ref[tile * N_EXPERTS + e], RUN_ALIGN)
        _pow2_pieces(n, tile_rows, lambda done, size: piece_fn(
            pl.multiple_of(local + done, RUN_ALIGN), pl.multiple_of(slot + done, RUN_ALIGN), size),
            rare_size=RARE_RUN)
        return local + n

    return lax.fori_loop(0, N_EXPERTS, per_expert, jnp.int32(0))


def _sorted_rows(tile_rows):
    return TOP_K * tile_rows + N_EXPERTS * RUN_ALIGN


def _sorted_onehot(rloc_ref, tile_rows):
    rows = lax.broadcasted_iota(jnp.int32, (_sorted_rows(tile_rows), tile_rows), 0)
    return [rows == rloc_ref[k:k + 1, :] for k in range(TOP_K)]


def _dispatch_kernel(pstart_ref, cnt_ref, car_ref, zstart_ref, nzero_ref, x_ref, rloc_ref, xs_hbm,
                     ybuf, zbuf, total_s, sem_z, sem_y):
    i = pl.program_id(0)
    td = x_ref.shape[0]
    slot = i % 2

    def wait_slot(s):
        def wait_piece(done, size):
            del done
            pltpu.make_async_copy(ybuf.at[s, pl.ds(0, size)], xs_hbm.at[pl.ds(0, size)],
                                  sem_y.at[s]).wait()

        _pow2_pieces(total_s[s], TOP_K * td, wait_piece)

    @pl.when(i == 0)
    def _():
        zbuf[...] = jnp.zeros_like(zbuf)

        def zcopy(start):
            dst = pl.ds(pl.multiple_of(start, MOE_BLK), MOE_BLK)
            return pltpu.make_async_copy(zbuf, xs_hbm.at[dst], sem_z)

        def issue(j, carry):
            zcopy(zstart_ref[j]).start()
            return carry

        def issue_tail(j, carry):
            zcopy(j * MOE_BLK).start()
            return carry

        def drain(j, carry):
            zcopy(0).wait()
            return carry

        n_blocks = xs_hbm.shape[0] // MOE_BLK
        lax.fori_loop(0, nzero_ref[0], issue, 0)
        lax.fori_loop(nzero_ref[1], n_blocks, issue_tail, 0)
        lax.fori_loop(0, nzero_ref[0] + (n_blocks - nzero_ref[1]), drain, 0)

    @pl.when(i >= 2)
    def _():
        wait_slot(slot)

    onehot = jnp.zeros((_sorted_rows(td), td), F32)
    for hit in _sorted_onehot(rloc_ref, td):
        onehot = jnp.where(hit, 1.0, onehot)
    ybuf[slot] = _pack_halves(_dot(onehot.astype(BF16), x_ref[...]))

    def send(local, dst, size):
        pltpu.make_async_copy(ybuf.at[slot, pl.ds(local, size)], xs_hbm.at[pl.ds(dst, size)],
                              sem_y.at[slot]).start()

    total_s[slot] = _for_each_run(i, cnt_ref, car_ref, pstart_ref, td, send)

    @pl.when(i == pl.num_programs(0) - 1)
    def _():
        wait_slot(slot)

        @pl.when(i >= 1)
        def _():
            wait_slot(1 - slot)


def _dispatch(pstart, cnt, car, zstart, nzero, xn, rloc, n_slots, td):
    t = xn.shape[0]
    half = D_MODEL // 2
    return pl.pallas_call(
        _dispatch_kernel,
        grid_spec=pltpu.PrefetchScalarGridSpec(
            num_scalar_prefetch=5,
            grid=(t // td,),
            in_specs=[pl.BlockSpec((td, D_MODEL), lambda i, *_: (i, 0)),
                      pl.BlockSpec((TOP_K, td), lambda i, *_: (0, i))],
            out_specs=pl.BlockSpec(memory_space=pl.ANY),
            scratch_shapes=[pltpu.VMEM((2, _sorted_rows(td), half), jnp.uint32),
                            pltpu.VMEM((MOE_BLK, half), jnp.uint32),
                            pltpu.SMEM((2,), jnp.int32),
                            pltpu.SemaphoreType.DMA,
                            pltpu.SemaphoreType.DMA((2,))]),
        out_shape=jax.ShapeDtypeStruct((n_slots, half), jnp.uint32),
        compiler_params=_cparams(("arbitrary",)),
        name="dispatch",
    )(pstart, cnt, car, zstart, nzero, xn, rloc)


def _expert_kernel(be_ref, na_ref, xs_ref, w1_ref, b1_ref, w2_ref, b2_ref, o_ref):
    i = pl.program_id(0)

    @pl.when(i < na_ref[0])
    def _():
        h = _dot(_unpack_halves(xs_ref[...]), w1_ref[0]) + b1_ref[0]
        glu = jnp.minimum(h[:, :D_FF], SWIGLU_LIMIT)
        lin = jnp.clip(h[:, D_FF:], -SWIGLU_LIMIT, SWIGLU_LIMIT)
        act = glu * _sigmoid(SWIGLU_ALPHA * glu) * (lin + 1.0)
        o = (_dot(act.astype(BF16), w2_ref[0]) + b2_ref[0]).astype(BF16).astype(F32)
        o_ref[...] = _pack_halves(o)

    @pl.when(i >= na_ref[0])
    def _():
        o_ref[...] = jnp.zeros_like(o_ref)


def _experts(block_e, n_active, xs, w1, b1, w2, b2):
    n_slots = xs.shape[0]
    return pl.pallas_call(
        _expert_kernel,
        grid_spec=pltpu.PrefetchScalarGridSpec(
            num_scalar_prefetch=2,
            grid=(n_slots // MOE_BLK,),
            in_specs=[pl.BlockSpec((MOE_BLK, D_MODEL // 2),
                                   lambda i, be, na: (jnp.minimum(i, na[0] - 1), 0)),
                      pl.BlockSpec((1, D_MODEL, 2 * D_FF), lambda i, be, na: (be[i], 0, 0)),
                      pl.BlockSpec((1, 1, 2 * D_FF), lambda i, be, na: (be[i], 0, 0)),
                      pl.BlockSpec((1, D_FF, D_MODEL), lambda i, be, na: (be[i], 0, 0)),
                      pl.BlockSpec((1, 1, D_MODEL), lambda i, be, na: (be[i], 0, 0))],
            out_specs=pl.BlockSpec((MOE_BLK, D_MODEL // 2), lambda i, be, na: (i, 0))),
        out_shape=jax.ShapeDtypeStruct((n_slots, D_MODEL // 2), jnp.uint32),
        compiler_params=_cparams(("arbitrary",)),
        name="experts",
    )(block_e, n_active, xs, w1, b1, w2, b2)


def _combine_kernel(pstart_ref, cnt_ref, car_ref, h_ref, rloc_ref, gate_ref, gf_ref, outs_hbm, o_ref,
                    ybuf, total_s, sem_y):
    i = pl.program_id(0)
    nt = pl.num_programs(0)
    tc = h_ref.shape[0]
    slot = i % 2

    def fetch(tile, s):
        def piece(local, src, size):
            pltpu.make_async_copy(outs_hbm.at[pl.ds(src, size)], ybuf.at[s, pl.ds(local, size)],
                                  sem_y.at[s]).start()

        total_s[s] = _for_each_run(tile, cnt_ref, car_ref, pstart_ref, tc, piece)

    @pl.when(i == 0)
    def _():
        ybuf[...] = jnp.zeros_like(ybuf)
        fetch(i, slot)

    @pl.when(i + 1 < nt)
    def _():
        fetch(i + 1, 1 - slot)

    hits = _sorted_onehot(rloc_ref, tc)
    gsel = jnp.zeros((_sorted_rows(tc), tc), F32)
    for k in range(TOP_K):
        gsel = jnp.where(hits[k], gate_ref[k:k + 1, :], gsel)
    g_t = gsel.T
    g0 = g_t.astype(BF16)
    g1 = (g_t - g0.astype(F32)).astype(BF16)

    def wait_piece(done, size):
        del done
        pltpu.make_async_copy(outs_hbm.at[pl.ds(0, size)], ybuf.at[slot, pl.ds(0, size)],
                              sem_y.at[slot]).wait()

    _pow2_pieces(total_s[slot], TOP_K * tc, wait_piece)
    yb = _unpack_halves(ybuf[slot])
    o_ref[...] = _rms(h_ref[...] + _dot(g0, yb) + _dot(g1, yb), gf_ref[...])


def _combine(pstart, cnt, car, h2d, rloc, gate, gf, outs, tc):
    t = h2d.shape[0]
    kt = pl.BlockSpec((TOP_K, tc), lambda i, *_: (0, i))
    return pl.pallas_call(
        _combine_kernel,
        grid_spec=pltpu.PrefetchScalarGridSpec(
            num_scalar_prefetch=3,
            grid=(t // tc,),
            in_specs=[pl.BlockSpec((tc, D_MODEL), lambda i, *_: (i, 0)), kt, kt,
                      pl.BlockSpec((1, D_MODEL), lambda i, *_: (0, 0)),
                      pl.BlockSpec(memory_space=pl.ANY)],
            out_specs=pl.BlockSpec((tc, D_MODEL), lambda i, *_: (i, 0)),
            scratch_shapes=[pltpu.VMEM((2, _sorted_rows(tc), D_MODEL // 2), jnp.uint32),
                            pltpu.SMEM((2,), jnp.int32),
                            pltpu.SemaphoreType.DMA((2,))]),
        out_shape=jax.ShapeDtypeStruct((t, D_MODEL), F32),
        compiler_params=_cparams(("arbitrary",)),
        name="combine",
    )(pstart, cnt, car, h2d, rloc, gate, gf, outs)


def _prep_weights(g_mix_norm, w_in, conv_w, conv_b, a_log_fwd, a_log_bwd, dt_bias_fwd, dt_bias_bwd,
                  d_skip, g_fourier_norm, g_ssd_norm, w_out, g_xattn_norm, g_mem_norm, w_q, w_k,
                  w_v, w_o, g_moe_norm, w_router, b_router, w_mlp1, b_mlp1, w_mlp2, b_mlp2, g_final):
    s1 = FOURIER_W
    s2 = s1 + SSD_W
    s3 = s2 + CONV_CH
    w = w_in[0]
    wdt = jnp.zeros((D_MODEL, 2 * LANES), F32)
    wdt = wdt.at[:, :SSD_HEADS].set(w[:, s3:s3 + SSD_HEADS])
    wdt = wdt.at[:, LANES:LANES + SSD_HEADS].set(w[:, s3 + SSD_HEADS:])
    pad_heads = lambda a, b: jnp.zeros((2, 1, LANES), F32).at[0, 0, :SSD_HEADS].set(a[0]).at[
        1, 0, :SSD_HEADS].set(b[0])
    head_major = lambda a: jnp.broadcast_to(jnp.swapaxes(a, 1, 2), (2, LANES, LANES))
    alog = pad_heads(a_log_fwd, a_log_bwd)
    dtb = pad_heads(dt_bias_fwd, dt_bias_bwd)
    dsk = jnp.zeros((2, 1, SSD_W), F32).at[0, 0].set(jnp.repeat(d_skip[0], SSD_HEAD_DIM))
    lower = np.tril(np.ones((CHUNK, CHUNK), np.float32))
    return dict(
        g_mix=g_mix_norm[0][None], wu=w[:, :s1].astype(BF16), wz=w[:, s1:s2].astype(BF16),
        wx=w[:, s2:s3].astype(BF16), wdt=wdt.astype(BF16), wdtt=wdt.T.astype(BF16),
        cs=_channel_dft_table(), conv_w=conv_w[0, :, 0, :], conv_b=conv_b[0][None],
        tri=jnp.asarray(np.stack([lower, lower.T])), alog=alog, alogt=head_major(alog),
        dtb=dtb, dtbt=head_major(dtb), dsk=dsk,
        g_four=g_fourier_norm[0][None], g_ssd=g_ssd_norm[0][None],
        w4=w_out[0, :FOURIER_W].astype(BF16), wy=w_out[0, FOURIER_W:].astype(BF16),
        g_x=g_xattn_norm[0][None], g_mem=g_mem_norm[0][None],
        wq=w_q[0].astype(BF16), wk=w_k[0].astype(BF16), wv=w_v[0].astype(BF16), wo=w_o[0].astype(BF16),
        g_moe=g_moe_norm[0][None], wr_t=w_router[0].T, br=b_router[0][:, None],
        w1=w_mlp1[0].astype(BF16), b1=b_mlp1[0][:, None, :], w2=w_mlp2[0].astype(BF16),
        b2=b_mlp2[0][:, None, :], g_final=g_final[None])


def _encode_group(x, mem, p):
    bsz, seqlen, _ = x.shape
    t = bsz * seqlen
    tm = min(TOKEN_TILE, seqlen // 2)
    assert seqlen % (2 * tm) == 0 and tm % MOE_TILE == 0 and seqlen % CHUNK == 0, x.shape

    ab, z, xbc, dt, dtt = _inproj(x, p["g_mix"], p["wu"], p["wz"], p["wx"], p["wdt"], p["wdtt"],
                                  p["cs"], p["conv_w"], p["conv_b"], tm)
    cos_t, nsin_t, cmid_t = _dft_tables(seqlen)
    nb = 4 if bsz % 4 == 0 else (2 if bsz % 2 == 0 else 1)
    four = _fourier(ab.reshape(bsz, seqlen, 2 * FOURIER_W), cos_t, nsin_t, cmid_t, p["g_four"], nb,
                    tm, min(DFT_TILE, seqlen))
    yf, yb = _ssd(xbc.reshape(bsz, seqlen, CONV_CH), dt.reshape(bsz, seqlen, 2 * LANES), dtt,
                  p["tri"], p["alog"], p["alogt"], p["dtb"], p["dtbt"], p["dsk"])

    k, v = _kv(mem, p["g_mem"], p["wk"], p["wv"])
    h, xn, rloc, gate, cnt, car = _post(x, four, yf, yb, z.reshape(bsz, seqlen, SSD_W), k, v, p, tm)
    h = h.reshape(t, D_MODEL)
    xn = xn.reshape(t, D_MODEL)

    td = MOE_TILE
    counts = jnp.sum(cnt, axis=1)
    padded = ((counts + MOE_BLK - 1) // MOE_BLK) * MOE_BLK
    pend = jnp.cumsum(padded)
    pstart = (pend - padded).astype(jnp.int32)
    max_rows = t * TOP_K + (t // td) * N_EXPERTS * (RUN_ALIGN - 1)
    n_blocks = -(-max_rows // MOE_BLK) + N_EXPERTS
    n_slots = n_blocks * MOE_BLK
    n_active = (pend[-1] // MOE_BLK).astype(jnp.int32)
    blk = jnp.arange(n_blocks, dtype=jnp.int32)
    be = jnp.sum(pend[None, :] <= (jnp.minimum(blk, n_active - 1) * MOE_BLK)[:, None], axis=1)
    be = jnp.minimum(be, N_EXPERTS - 1).astype(jnp.int32)
    zstart = (pend - MOE_BLK).astype(jnp.int32)[jnp.argsort(padded == 0)]
    nzero = jnp.stack([jnp.sum(padded > 0).astype(jnp.int32), n_active])

    cnt_flat = cnt.T.reshape(-1)
    car_flat = car.T.reshape(-1)
    xs = _dispatch(pstart, cnt_flat, car_flat, zstart, nzero, xn, rloc, n_slots, td)
    outs = _experts(be, n_active[None], xs, p["w1"], p["b1"], p["w2"], p["b2"])
    y = _combine(pstart, cnt_flat, car_flat, h, rloc, gate, p["g_final"], outs, td)
    return y.reshape(bsz, seqlen, D_MODEL)


def kernel(x_prompt, x_sample, mem_prompt, mem_sample, g_mix_norm, w_in, conv_w, conv_b, a_log_fwd, a_log_bwd, dt_bias_fwd, dt_bias_bwd, d_skip, g_fourier_norm, g_ssd_norm, w_out, g_xattn_norm, g_mem_norm, w_q, w_k, w_v, w_o, g_moe_norm, w_router, b_router, w_mlp1, b_mlp1, w_mlp2, b_mlp2, g_final):
    p = _prep_weights(g_mix_norm, w_in, conv_w, conv_b, a_log_fwd, a_log_bwd, dt_bias_fwd,
                      dt_bias_bwd, d_skip, g_fourier_norm, g_ssd_norm, w_out, g_xattn_norm,
                      g_mem_norm, w_q, w_k, w_v, w_o, g_moe_norm, w_router, b_router, w_mlp1,
                      b_mlp1, w_mlp2, b_mlp2, g_final)
    return (_encode_group(x_prompt, mem_prompt, p), _encode_group(x_sample, mem_sample, p))
```

```python
import functools

import numpy as np
import jax
import jax.numpy as jnp
from jax import lax
from jax.experimental import pallas as pl
from jax.experimental.pallas import tpu as pltpu

F32 = jnp.float32
BF16 = jnp.bfloat16

D_MODEL = 1024
D_MIX = 2 * D_MODEL
FOURIER_W = D_MIX // 4
FOURIER_GROUPS = 8
FOURIER_GROUP_DIM = FOURIER_W // FOURIER_GROUPS
SSD_W = D_MIX - FOURIER_W
SSD_HEAD_DIM = 64
SSD_HEADS = SSD_W // SSD_HEAD_DIM
SSD_GROUPS = 4
SSD_HPG = SSD_HEADS // SSD_GROUPS
D_STATE = 128
CONV_WIDTH = 5
CONV_PAD = CONV_WIDTH // 2
GN = SSD_GROUPS * D_STATE
CONV_CH = SSD_W + 2 * GN
CHUNK = 128
SSD_SUB = 4
XATTN_HEADS = 4
XATTN_HEAD_DIM = D_MODEL // XATTN_HEADS
N_EXPERTS = 32
TOP_K = 4
D_FF = D_MODEL
SWIGLU_ALPHA = 1.702
SWIGLU_LIMIT = 7.0
NORM_EPS = 1e-5
LOG2E = 1.4426950408889634

LANES = 128
HALO = 8
CONV_COLS = 512
TOKEN_TILE = 512
DFT_TILE = 1024
CARRY_ROWS = 16
MOE_BLK = 512
MOE_TILE = 256
RUN_ALIGN = 8
RARE_RUN = 64
VMEM_LIMIT = 48 * 1024 * 1024


def _cparams(sem):
    return pltpu.CompilerParams(dimension_semantics=sem, vmem_limit_bytes=VMEM_LIMIT)


def _rms(x, g):
    return x * lax.rsqrt(jnp.mean(x * x, axis=-1, keepdims=True) + NORM_EPS) * g


def _sigmoid(x):
    return 1.0 / (1.0 + jnp.exp(-x))


def _softplus(x):
    return jnp.maximum(x, 0.0) + jnp.log1p(jnp.exp(-jnp.abs(x)))


def _dot(a, b):
    return jnp.dot(a, b, preferred_element_type=F32)


def _dot_nt(a, b):
    return lax.dot_general(a, b, (((1,), (1,)), ((), ())), preferred_element_type=F32)


def _pack_halves(x):
    c = x.shape[1] // 2
    u = lax.bitcast_convert_type(x, jnp.uint32)
    return (u[:, :c] >> 16) | (u[:, c:] & jnp.uint32(0xFFFF0000))


def _unpack_halves(w):
    lo = lax.bitcast_convert_type(w << 16, F32)
    hi = lax.bitcast_convert_type(w & jnp.uint32(0xFFFF0000), F32)
    return jnp.concatenate([lo, hi], axis=1).astype(BF16)


def _split3(x):
    h = x.astype(BF16)
    r = x - h.astype(F32)
    m = r.astype(BF16)
    l = (r - m.astype(F32)).astype(BF16)
    return h, m, l


def _inproj_kernel(prev_ref, x_ref, next_ref, g_ref, wu_ref, wz_ref, wx_ref, wdt_ref, wdtt_ref,
                   cs_ref, cw_ref, cb_ref, ab_ref, z_ref, xbc_ref, dt_ref, dtt_ref):
    i = pl.program_id(1)
    tm = x_ref.shape[1]
    rows = tm + 2 * HALO
    mid = slice(HALO, HALO + tm)
    xa = jnp.concatenate([prev_ref[0], x_ref[0], next_ref[0]], axis=0)
    xn = _rms(xa, g_ref[...]).astype(BF16)
    u = _dot(xn, wu_ref[...])[mid]
    ab_ref[...] = _dot(u.astype(BF16), cs_ref[...]).astype(BF16)
    z_ref[...] = _dot(xn, wz_ref[...])[mid].astype(BF16)
    dt_ref[...] = _dot(xn, wdt_ref[...])[mid]
    dtt_ref[...] = _dot_nt(wdtt_ref[...], xn)[:, mid]

    r = lax.broadcasted_iota(jnp.int32, (rows, 1), 0)
    has_prev = jnp.where(i > 0, 1.0, 0.0)
    has_next = jnp.where(i < pl.num_programs(1) - 1, 1.0, 0.0)
    inside = jnp.where(r < HALO, has_prev, jnp.where(r >= HALO + tm, has_next, 1.0))
    for c in range(CONV_CH // CONV_COLS):
        cols = slice(c * CONV_COLS, (c + 1) * CONV_COLS)
        xe = _dot(xn, wx_ref[:, cols]) * inside
        acc = cb_ref[:, cols] + cw_ref[CONV_PAD:CONV_PAD + 1, cols] * xe
        for w in range(CONV_WIDTH):
            if w != CONV_PAD:
                acc = acc + cw_ref[w:w + 1, cols] * pltpu.roll(xe, (CONV_PAD - w) % rows, axis=0)
        y = acc[mid]
        xbc_ref[:, cols] = (y * _sigmoid(y)).astype(BF16)


def _inproj(x, g, wu, wz, wx, wdt, wdtt, cs, cw, cb, tm):
    bsz, seqlen, _ = x.shape
    t = bsz * seqlen
    nt = seqlen // tm
    r = tm // HALO
    nh = seqlen // HALO
    full = lambda a: pl.BlockSpec(a.shape, lambda b, i: (0,) * a.ndim)
    row = lambda w: pl.BlockSpec((tm, w), lambda b, i: (b * nt + i, 0))
    return pl.pallas_call(
        _inproj_kernel,
        grid=(bsz, nt),
        in_specs=[pl.BlockSpec((1, HALO, D_MODEL), lambda b, i: (b, jnp.maximum(i * r - 1, 0), 0)),
                  pl.BlockSpec((1, tm, D_MODEL), lambda b, i: (b, i, 0)),
                  pl.BlockSpec((1, HALO, D_MODEL), lambda b, i: (b, jnp.minimum((i + 1) * r, nh - 1), 0)),
                  full(g), full(wu), full(wz), full(wx), full(wdt), full(wdtt), full(cs), full(cw),
                  full(cb)],
        out_specs=[row(2 * FOURIER_W), row(SSD_W), row(CONV_CH), row(2 * LANES),
                   pl.BlockSpec((2 * LANES, tm), lambda b, i: (0, b * nt + i))],
        out_shape=[jax.ShapeDtypeStruct((t, 2 * FOURIER_W), BF16),
                   jax.ShapeDtypeStruct((t, SSD_W), BF16),
                   jax.ShapeDtypeStruct((t, CONV_CH), BF16),
                   jax.ShapeDtypeStruct((t, 2 * LANES), F32),
                   jax.ShapeDtypeStruct((2 * LANES, t), F32)],
        compiler_params=_cparams(("parallel", "parallel")),
        name="inproj",
    )(x, x, x, g, wu, wz, wx, wdt, wdtt, cs, cw, cb)


def _fourier_kernel(c_ref, s_ref, cm_ref, ab_ref, g_ref, o_ref, accp_ref, accq_ref, accm_ref,
                    carry_ref, *, scale, nb):
    j = pl.program_id(1)
    k = pl.program_id(2)
    tr = c_ref.shape[0]

    @pl.when(k == 0)
    def _():
        accp_ref[...] = jnp.zeros_like(accp_ref)
        accq_ref[...] = jnp.zeros_like(accq_ref)

    @pl.when((k == 0) & (j == 0))
    def _():
        accm_ref[...] = jnp.zeros_like(accm_ref)

    c = c_ref[...]
    s = s_ref[...]
    for b in range(nb):
        accp_ref[b] += _dot(c, ab_ref[b, :, :FOURIER_W])
        accq_ref[b] += _dot(s, ab_ref[b, :, FOURIER_W:])

    @pl.when(j == 0)
    def _():
        for b in range(nb):
            accm_ref[b] += _dot(cm_ref[...], ab_ref[b, :, :FOURIER_W])

    @pl.when(k == pl.num_programs(2) - 1)
    def _():
        g = g_ref[...]
        r = lax.broadcasted_iota(jnp.int32, (tr, tr + CARRY_ROWS), 0)
        col = lax.broadcasted_iota(jnp.int32, (tr, tr + CARRY_ROWS), 1)
        flip = jnp.where(col == jnp.where(r == 0, tr, tr - r), 1.0, 0.0).astype(BF16)
        for b in range(nb):
            @pl.when(j == 0)
            def _():
                carry_ref[b] = _rms(accm_ref[b] * scale, g).astype(BF16)

            p = accp_ref[b] * scale
            q = accq_ref[b] * scale
            o_ref[b, 0] = _rms(p + q, g).astype(BF16)
            m = _rms(p - q, g).astype(BF16)
            o_ref[b, 1] = _dot(flip, jnp.concatenate([m, carry_ref[b]], axis=0)).astype(BF16)
            carry_ref[b] = m[:CARRY_ROWS]


def _fourier(ab, cos_t, nsin_t, cmid_t, g, nb, tr, tk):
    bsz, seqlen, _ = ab.shape
    half_tiles = seqlen // (2 * tr)
    scale = float(1.0 / np.sqrt(seqlen * FOURIER_GROUP_DIM))
    tile = lambda b, j, k: (half_tiles - 1 - j, k)
    return pl.pallas_call(
        functools.partial(_fourier_kernel, scale=scale, nb=nb),
        grid=(bsz // nb, half_tiles, seqlen // tk),
        in_specs=[pl.BlockSpec((tr, tk), tile),
                  pl.BlockSpec((tr, tk), tile),
                  pl.BlockSpec((CARRY_ROWS, tk), lambda b, j, k: (0, k)),
                  pl.BlockSpec((nb, tk, 2 * FOURIER_W), lambda b, j, k: (b, k, 0)),
                  pl.BlockSpec((1, FOURIER_W), lambda b, j, k: (0, 0))],
        out_specs=pl.BlockSpec((nb, 2, tr, FOURIER_W), lambda b, j, k: (b, 0, half_tiles - 1 - j, 0)),
        out_shape=jax.ShapeDtypeStruct((bsz, 2, seqlen // 2, FOURIER_W), BF16),
        scratch_shapes=[pltpu.VMEM((nb, tr, FOURIER_W), F32),
                        pltpu.VMEM((nb, tr, FOURIER_W), F32),
                        pltpu.VMEM((nb, CARRY_ROWS, FOURIER_W), F32),
                        pltpu.VMEM((nb, CARRY_ROWS, FOURIER_W), BF16)],
        compiler_params=_cparams(("parallel", "arbitrary", "arbitrary")),
        name="fourier",
    )(cos_t, nsin_t, cmid_t, ab, g)


def _dft_tables(seqlen):
    lo = min(seqlen // 2, LANES)
    hi = seqlen // 2 // lo
    l = np.arange(seqlen, dtype=np.int64)
    a_hi = 2.0 * np.pi * ((np.arange(hi, dtype=np.int64)[:, None] * lo * l[None, :]) % seqlen) / seqlen
    a_lo = 2.0 * np.pi * ((np.arange(lo, dtype=np.int64)[:, None] * l[None, :]) % seqlen) / seqlen
    c1 = jnp.asarray(np.cos(a_hi), F32)[:, None, :]
    s1 = jnp.asarray(np.sin(a_hi), F32)[:, None, :]
    c0 = jnp.asarray(np.cos(a_lo), F32)[None, :, :]
    s0 = jnp.asarray(np.sin(a_lo), F32)[None, :, :]
    cos_t = (c1 * c0 - s1 * s0).reshape(seqlen // 2, seqlen).astype(BF16)
    nsin_t = (-(s1 * c0 + c1 * s0)).reshape(seqlen // 2, seqlen).astype(BF16)
    cmid = np.zeros((CARRY_ROWS, seqlen), np.float32)
    cmid[0] = 1.0 - 2.0 * (l % 2)
    return cos_t, nsin_t, jnp.asarray(cmid, BF16)


def _channel_dft_table():
    c = np.arange(FOURIER_W)
    same = (c[:, None] // FOURIER_GROUP_DIM) == (c[None, :] // FOURIER_GROUP_DIM)
    ang = 2.0 * np.pi * ((c[:, None] % FOURIER_GROUP_DIM) * (c[None, :] % FOURIER_GROUP_DIM)
                         % FOURIER_GROUP_DIM) / FOURIER_GROUP_DIM
    cc = np.where(same, np.cos(ang), 0.0)
    sc = np.where(same, np.sin(ang), 0.0)
    return jnp.asarray(np.concatenate([cc, sc], axis=1), BF16)


def _ssd_kernel(xf_ref, xb_ref, dtf_ref, dtb_ref, dttf_ref, dttb_ref, tri_ref, alog_ref, alogt_ref,
                bias_ref, biast_ref, dsk_ref, yf_ref, yb_ref, state_ref):
    @pl.when(pl.program_id(1) == 0)
    def _():
        state_ref[...] = jnp.zeros_like(state_ref)

    for c in range(SSD_SUB):
        for d, (x_ref, dt_ref, dtt_ref, y_ref) in enumerate(((xf_ref, dtf_ref, dttf_ref, yf_ref),
                                                             (xb_ref, dtb_ref, dttb_ref, yb_ref))):
            sub = c if d == 0 else SSD_SUB - 1 - c
            _ssd_chunk(slice(sub * CHUNK, (sub + 1) * CHUNK), x_ref, dt_ref, dtt_ref, tri_ref.at[d],
                       tri_ref.at[1 - d], alog_ref.at[d], alogt_ref.at[d], bias_ref.at[d],
                       biast_ref.at[d], dsk_ref.at[d], y_ref, state_ref.at[d])


def _ssd_chunk(rows, xbc_ref, dt_ref, dtt_ref, tri_ref, trit_ref, alog_ref, alogt_ref, dtb_ref,
               dtbt_ref, dsk_ref, y_ref, state_ref):
    q = CHUNK
    tri = tri_ref[...]
    mask = tri > 0.5
    dt = _softplus(dt_ref[0, rows, :] + dtb_ref[...])
    d0, d1, d2 = _split3(dt * -jnp.exp(alog_ref[...]))
    tri_b = tri.astype(BF16)
    acum2 = (_dot(tri_b, d0) + _dot(tri_b, d1) + _dot(tri_b, d2)) * LOG2E
    dt_t = _softplus(dtt_ref[:, rows] + dtbt_ref[...])
    t0, t1, t2 = _split3(dt_t * -jnp.exp(alogt_ref[...]))
    trit_b = trit_ref[...].astype(BF16)
    ones_b = jnp.ones((q, LANES), BF16)
    acum_t2 = (_dot(t0, trit_b) + _dot(t1, trit_b) + _dot(t2, trit_b)) * LOG2E
    tot2 = (_dot(t0, ones_b) + _dot(t1, ones_b) + _dot(t2, ones_b)) * LOG2E
    row2 = acum_t2 - jnp.log2(dt_t)
    w_t = dt_t * jnp.exp2(tot2 - acum_t2)
    etot = jnp.exp2(tot2)
    lane_lo = lax.broadcasted_iota(jnp.int32, (1, LANES), 1) < SSD_HEAD_DIM

    for g in range(SSD_GROUPS):
        bm = xbc_ref[0, rows, SSD_W + g * D_STATE:SSD_W + (g + 1) * D_STATE]
        cm = xbc_ref[0, rows, SSD_W + GN + g * D_STATE:SSD_W + GN + (g + 1) * D_STATE]
        cb = _dot_nt(cm, bm)
        cm_f = cm.astype(F32)
        bm_t = bm.astype(F32).T
        for p in range(SSD_HPG // 2):
            t = g * (SSD_HPG // 2) + p
            cols = slice(t * LANES, (t + 1) * LANES)
            xs = xbc_ref[0, rows, cols]
            st = state_ref[:, cols]
            lhs = []
            lhs2 = []
            for h in (2 * t, 2 * t + 1):
                colb2 = jnp.broadcast_to(acum2[:, h:h + 1], (q, q))
                decay_dt = jnp.where(mask, jnp.exp2(colb2 - row2[h:h + 1, :]), 0.0)
                lhs.append(jnp.concatenate([(cb * decay_dt).astype(BF16),
                                            (cm_f * jnp.exp2(colb2)).astype(BF16)], axis=1))
                lhs2.append((bm_t * w_t[h:h + 1, :]).astype(BF16))
            yy = _dot(jnp.concatenate(lhs, axis=0), jnp.concatenate([xs, st.astype(BF16)], axis=0))
            y = jnp.where(lane_lo, yy[:q], yy[q:]) + dsk_ref[:, cols] * xs.astype(F32)
            y_ref[0, rows, cols] = y.astype(BF16)
            dec = jnp.where(lane_lo, etot[2 * t:2 * t + 1, :], etot[2 * t + 1:2 * t + 2, :])
            ss = _dot(jnp.concatenate(lhs2, axis=0), xs)
            state_ref[:, cols] = st * dec + jnp.where(lane_lo, ss[:D_STATE], ss[D_STATE:])


def _ssd(xbc, dt, dtt, tri, alog, alogt, dtb, dtbt, dsk):
    bsz, seqlen, _ = xbc.shape
    rows = SSD_SUB * CHUNK
    nc = seqlen // rows
    full = lambda a: pl.BlockSpec(a.shape, lambda b, j: (0,) * a.ndim)
    return pl.pallas_call(
        _ssd_kernel,
        grid=(bsz, nc),
        in_specs=[pl.BlockSpec((1, rows, CONV_CH), lambda b, j: (b, j, 0)),
                  pl.BlockSpec((1, rows, CONV_CH), lambda b, j: (b, nc - 1 - j, 0)),
                  pl.BlockSpec((1, rows, LANES), lambda b, j: (b, j, 0)),
                  pl.BlockSpec((1, rows, LANES), lambda b, j: (b, nc - 1 - j, 1)),
                  pl.BlockSpec((LANES, rows), lambda b, j: (0, b * nc + j)),
                  pl.BlockSpec((LANES, rows), lambda b, j: (1, b * nc + nc - 1 - j)),
                  full(tri), full(alog), full(alogt), full(dtb), full(dtbt), full(dsk)],
        out_specs=[pl.BlockSpec((1, rows, SSD_W), lambda b, j: (b, j, 0)),
                   pl.BlockSpec((1, rows, SSD_W), lambda b, j: (b, nc - 1 - j, 0))],
        out_shape=[jax.ShapeDtypeStruct((bsz, seqlen, SSD_W), BF16)] * 2,
        scratch_shapes=[pltpu.VMEM((2, D_STATE, SSD_W), F32)],
        compiler_params=_cparams(("parallel", "arbitrary")),
        name="ssd",
    )(xbc, xbc, dt, dt, dtt, dtt, tri, alog, alogt, dtb, dtbt, dsk)


def _kv_kernel(mem_ref, g_ref, wk_ref, wv_ref, k_ref, v_ref):
    mn = _rms(mem_ref[0], g_ref[...]).astype(BF16)
    k_ref[0] = _dot(mn, wk_ref[...]).astype(BF16)
    v_ref[0] = _dot(mn, wv_ref[...]).astype(BF16)


def _kv(mem, g, wk, wv):
    bsz, nm, _ = mem.shape
    full = lambda a: pl.BlockSpec(a.shape, lambda b: (0,) * a.ndim)
    blk = pl.BlockSpec((1, nm, D_MODEL), lambda b: (b, 0, 0))
    return pl.pallas_call(
        _kv_kernel,
        grid=(bsz,),
        in_specs=[blk, full(g), full(wk), full(wv)],
        out_specs=[blk, blk],
        out_shape=[jax.ShapeDtypeStruct(mem.shape, BF16)] * 2,
        compiler_params=_cparams(("parallel",)),
        name="kv",
    )(mem, g, wk, wv)


def _cross_attend(h, g, wq_ref, k_ref, v_ref, wo_ref):
    hn = _rms(h, g).astype(BF16)
    q = (_dot(hn, wq_ref[...]) * (XATTN_HEAD_DIM ** -0.5)).astype(BF16)
    outs = []
    for hd in range(XATTN_HEADS):
        cols = slice(hd * XATTN_HEAD_DIM, (hd + 1) * XATTN_HEAD_DIM)
        s = _dot_nt(q[:, cols], k_ref[0, :, cols])
        e = jnp.exp(s - jnp.max(s, axis=-1, keepdims=True))
        p = (e / jnp.sum(e, axis=-1, keepdims=True)).astype(BF16)
        outs.append(_dot(p, v_ref[0, :, cols]).astype(BF16))
    return h + _dot(jnp.concatenate(outs, axis=1), wo_ref[...])


def _route_tile(xn, i, cols, w0, w1, br, rloc_ref, gate_ref, carry_ref, cnt_acc, car_acc):
    tm = xn.shape[0]
    x0, x1, _ = _split3(xn)
    logits = _dot_nt(w0, x0) + _dot_nt(w0, x1) + _dot_nt(w1, x0) + br

    eid = lax.broadcasted_iota(jnp.int32, (N_EXPERTS, tm), 0)
    work = logits
    vals, hits = [], []
    sel = jnp.zeros((N_EXPERTS, tm), F32)
    for k in range(TOP_K):
        m = jnp.max(work, axis=0, keepdims=True)
        ik = jnp.min(jnp.where(work == m, eid, N_EXPERTS), axis=0, keepdims=True)
        hit = eid == ik
        vals.append(m)
        hits.append(hit)
        sel = jnp.where(hit, 1.0, sel)
        work = jnp.where(hit, -jnp.inf, work)
    es = [jnp.exp(v - vals[0]) for v in vals]
    den = es[0] + es[1] + es[2] + es[3]
    for k in range(TOP_K):
        gate_ref[k:k + 1, cols] = es[k] / den

    r = lax.broadcasted_iota(jnp.int32, (tm, tm), 0)
    c = lax.broadcasted_iota(jnp.int32, (tm, tm), 1)
    before = jnp.where(r < c, 1.0, 0.0).astype(BF16)
    cum = _dot(sel.astype(BF16), before)
    counts = jnp.sum(sel, axis=1, keepdims=True)
    counts = jnp.floor((counts + (RUN_ALIGN - 1)) * (1.0 / RUN_ALIGN)) * RUN_ALIGN
    er = lax.broadcasted_iota(jnp.int32, (N_EXPERTS, N_EXPERTS), 0)
    ec = lax.broadcasted_iota(jnp.int32, (N_EXPERTS, N_EXPERTS), 1)
    lower = jnp.where(er > ec, 1.0, 0.0).astype(BF16)
    offs = _dot(lower, jnp.broadcast_to(counts, (N_EXPERTS, LANES)).astype(BF16))[:, 0:1]
    pos = cum + offs
    for k in range(TOP_K):
        rk = jnp.sum(jnp.where(hits[k], pos, 0.0), axis=0, keepdims=True)
        rloc_ref[k:k + 1, cols] = rk.astype(jnp.int32)

    lane = lax.broadcasted_iota(jnp.int32, cnt_acc.shape, 1)
    cnt_acc[...] = jnp.where(lane == i, counts, cnt_acc[...])
    car_acc[...] = jnp.where(lane == i, carry_ref[:, 0:1], car_acc[...])
    carry_ref[...] = carry_ref[...] + counts


def _post_kernel(x_ref, four_ref, yf_ref, yb_ref, z_ref, gs_ref, w4_ref, wy_ref, gx_ref, wq_ref, k_ref, v_ref,
                 wo_ref, gm_ref, wr_ref, br_ref, h_ref, xn_ref, rloc_ref, gate_ref, cnt_ref, car_ref,
                 carry_ref, cnt_acc, car_acc):
    step = pl.program_id(0) * pl.num_programs(1) + pl.program_id(1)
    tm = x_ref.shape[1]

    @pl.when(step == 0)
    def _():
        carry_ref[...] = jnp.zeros_like(carry_ref)
        cnt_acc[...] = jnp.zeros_like(cnt_acc)
        car_acc[...] = jnp.zeros_like(car_acc)

    y = yf_ref[0].astype(F32) + yb_ref[0].astype(F32)
    z = z_ref[0].astype(F32)
    yn = _rms(y * (z * _sigmoid(z)), gs_ref[...]).astype(BF16)
    h = x_ref[0] + _dot(four_ref[0, 0], w4_ref[...]) + _dot(yn, wy_ref[...])

    h = _cross_attend(h, gx_ref[...], wq_ref, k_ref, v_ref, wo_ref)
    h_ref[0] = h

    xn = _rms(h, gm_ref[...])
    xn_ref[0] = xn.astype(BF16)
    w0, w1, _ = _split3(wr_ref[...])
    sub = tm // MOE_TILE
    for s in range(sub):
        rows = slice(s * MOE_TILE, (s + 1) * MOE_TILE)
        _route_tile(xn[rows], step * sub + s, rows, w0, w1, br_ref[...], rloc_ref, gate_ref,
                    carry_ref, cnt_acc, car_acc)
    cnt_ref[...] = cnt_acc[...].astype(jnp.int32)
    car_ref[...] = car_acc[...].astype(jnp.int32)


def _post(x, four, yf, yb, z, mem_k, mem_v, p, tm):
    bsz, seqlen, _ = x.shape
    t = bsz * seqlen
    nt = seqlen // tm
    n_tiles = t // MOE_TILE
    nm = mem_k.shape[1]
    weights = [p["g_ssd"], p["w4"], p["wy"], p["g_x"], p["wq"]]
    weights2 = [p["wo"], p["g_moe"], p["wr_t"], p["br"]]
    full = lambda a: pl.BlockSpec(a.shape, lambda b, i: (0,) * a.ndim)
    tok = lambda w: pl.BlockSpec((1, tm, w), lambda b, i: (b, i, 0))
    mem = pl.BlockSpec((1, nm, D_MODEL), lambda b, i: (b, 0, 0))
    kt = pl.BlockSpec((TOP_K, tm), lambda b, i: (0, b * nt + i))
    per_tile = pl.BlockSpec((N_EXPERTS, n_tiles), lambda b, i: (0, 0))
    return pl.pallas_call(
        _post_kernel,
        grid=(bsz, nt),
        in_specs=[tok(D_MODEL),
                  pl.BlockSpec((1, 1, tm, FOURIER_W), lambda b, i: (
                      b, i // (nt // 2), jnp.where(i < nt // 2, i, nt - 1 - i), 0)),
                  tok(SSD_W), tok(SSD_W), tok(SSD_W)]
                 + [full(a) for a in weights] + [mem, mem] + [full(a) for a in weights2],
        out_specs=[tok(D_MODEL), tok(D_MODEL), kt, kt, per_tile, per_tile],
        out_shape=[jax.ShapeDtypeStruct((bsz, seqlen, D_MODEL), F32),
                   jax.ShapeDtypeStruct((bsz, seqlen, D_MODEL), BF16),
                   jax.ShapeDtypeStruct((TOP_K, t), jnp.int32),
                   jax.ShapeDtypeStruct((TOP_K, t), F32),
                   jax.ShapeDtypeStruct((N_EXPERTS, n_tiles), jnp.int32),
                   jax.ShapeDtypeStruct((N_EXPERTS, n_tiles), jnp.int32)],
        scratch_shapes=[pltpu.VMEM((N_EXPERTS, LANES), F32),
                        pltpu.VMEM((N_EXPERTS, n_tiles), F32),
                        pltpu.VMEM((N_EXPERTS, n_tiles), F32)],
        compiler_params=_cparams(("arbitrary", "arbitrary")),
        name="post",
    )(x, four, yf, yb, z, *weights, mem_k, mem_v, *weights2)


def _pow2_pieces(n, max_size, piece_fn, rare_size=None):
    def pieces(sizes, done):
        for size in sizes:
            bit = n & size

            @pl.when(bit != 0)
            def _(size=size, done=done):
                piece_fn(done, size)

            done = done + bit

    sizes = []
    size = max_size
    while size >= RUN_ALIGN:
        sizes.append(size)
        size //= 2
    rare = [s for s in sizes if rare_size is not None and s >= rare_size]
    if rare:
        pl.when(n >= rare_size)(lambda: pieces(rare, jnp.int32(0)))
    pieces(sizes[len(rare):], n & -rare[-1] if rare else jnp.int32(0))


def _for_each_run(tile, cnt_ref, car_ref, pstart_ref, tile_rows, piece_fn):
    def per_expert(e, local):
        n = cnt_ref[tile * N_EXPERTS + e]
        local = pl.multiple_of(local, RUN_ALIGN)
        slot = pl.multiple_of(pstart_ref[e] + car_ref[tile * N_EXPERTS + e], RUN_ALIGN)
        _pow2_pieces(n, tile_rows, lambda done, size: piece_fn(
            pl.multiple_of(local + done, RUN_ALIGN), pl.multiple_of(slot + done, RUN_ALIGN), size),
            rare_size=RARE_RUN)
        return local + n

    return lax.fori_loop(0, N_EXPERTS, per_expert, jnp.int32(0))


def _sorted_rows(tile_rows):
    return TOP_K * tile_rows + N_EXPERTS * RUN_ALIGN


def _sorted_onehot(rloc_ref, tile_rows):
    rows = lax.broadcasted_iota(jnp.int32, (_sorted_rows(tile_rows), tile_rows), 0)
    return [rows == rloc_ref[k:k + 1, :] for k in range(TOP_K)]


def _dispatch_kernel(pstart_ref, cnt_ref, car_ref, zstart_ref, nzero_ref, x_ref, rloc_ref, xs_hbm,
                     ybuf, zbuf, total_s, sem_z, sem_y):
    i = pl.program_id(0)
    td = x_ref.shape[0]
    slot = i % 2

    def wait_slot(s):
        def wait_piece(done, size):
            del done
            pltpu.make_async_copy(ybuf.at[s, pl.ds(0, size)], xs_hbm.at[pl.ds(0, size)],
                                  sem_y.at[s]).wait()

        _pow2_pieces(total_s[s], TOP_K * td, wait_piece)

    @pl.when(i == 0)
    def _():
        zbuf[...] = jnp.zeros_like(zbuf)

        def zcopy(start):
            dst = pl.ds(pl.multiple_of(start, MOE_BLK), MOE_BLK)
            return pltpu.make_async_copy(zbuf, xs_hbm.at[dst], sem_z)

        def issue(j, carry):
            zcopy(zstart_ref[j]).start()
            return carry

        def issue_tail(j, carry):
            zcopy(j * MOE_BLK).start()
            return carry

        def drain(j, carry):
            zcopy(0).wait()
            return carry

        n_blocks = xs_hbm.shape[0] // MOE_BLK
        lax.fori_loop(0, nzero_ref[0], issue, 0)
        lax.fori_loop(nzero_ref[1], n_blocks, issue_tail, 0)
        lax.fori_loop(0, nzero_ref[0] + (n_blocks - nzero_ref[1]), drain, 0)

    @pl.when(i >= 2)
    def _():
        wait_slot(slot)

    onehot = jnp.zeros((_sorted_rows(td), td), F32)
    for hit in _sorted_onehot(rloc_ref, td):
        onehot = jnp.where(hit, 1.0, onehot)
    ybuf[slot] = _pack_halves(_dot(onehot.astype(BF16), x_ref[...]))

    def send(local, dst, size):
        pltpu.make_async_copy(ybuf.at[slot, pl.ds(local, size)], xs_hbm.at[pl.ds(dst, size)],
                              sem_y.at[slot]).start()

    total_s[slot] = _for_each_run(i, cnt_ref, car_ref, pstart_ref, td, send)

    @pl.when(i == pl.num_programs(0) - 1)
    def _():
        wait_slot(slot)

        @pl.when(i >= 1)
        def _():
            wait_slot(1 - slot)


def _dispatch(pstart, cnt, car, zstart, nzero, xn, rloc, n_slots, td):
    t = xn.shape[0]
    half = D_MODEL // 2
    return pl.pallas_call(
        _dispatch_kernel,
        grid_spec=pltpu.PrefetchScalarGridSpec(
            num_scalar_prefetch=5,
            grid=(t // td,),
            in_specs=[pl.BlockSpec((td, D_MODEL), lambda i, *_: (i, 0)),
                      pl.BlockSpec((TOP_K, td), lambda i, *_: (0, i))],
            out_specs=pl.BlockSpec(memory_space=pl.ANY),
            scratch_shapes=[pltpu.VMEM((2, _sorted_rows(td), half), jnp.uint32),
                            pltpu.VMEM((MOE_BLK, half), jnp.uint32),
                            pltpu.SMEM((2,), jnp.int32),
                            pltpu.SemaphoreType.DMA,
                            pltpu.SemaphoreType.DMA((2,))]),
        out_shape=jax.ShapeDtypeStruct((n_slots, half), jnp.uint32),
        compiler_params=_cparams(("arbitrary",)),
        name="dispatch",
    )(pstart, cnt, car, zstart, nzero, xn, rloc)


def _expert_kernel(be_ref, na_ref, xs_ref, w1_ref, b1_ref, w2_ref, b2_ref, o_ref):
    i = pl.program_id(0)

    @pl.when(i < na_ref[0])
    def _():
        h = _dot(_unpack_halves(xs_ref[...]), w1_ref[0]) + b1_ref[0]
        glu = jnp.minimum(h[:, :D_FF], SWIGLU_LIMIT)
        lin = jnp.clip(h[:, D_FF:], -SWIGLU_LIMIT, SWIGLU_LIMIT)
        act = glu * _sigmoid(SWIGLU_ALPHA * glu) * (lin + 1.0)
        o = (_dot(act.astype(BF16), w2_ref[0]) + b2_ref[0]).astype(BF16).astype(F32)
        o_ref[...] = _pack_halves(o)

    @pl.when(i >= na_ref[0])
    def _():
        o_ref[...] = jnp.zeros_like(o_ref)


def _experts(block_e, n_active, xs, w1, b1, w2, b2):
    n_slots = xs.shape[0]
    return pl.pallas_call(
        _expert_kernel,
        grid_spec=pltpu.PrefetchScalarGridSpec(
            num_scalar_prefetch=2,
            grid=(n_slots // MOE_BLK,),
            in_specs=[pl.BlockSpec((MOE_BLK, D_MODEL // 2),
                                   lambda i, be, na: (jnp.minimum(i, na[0] - 1), 0)),
                      pl.BlockSpec((1, D_MODEL, 2 * D_FF), lambda i, be, na: (be[i], 0, 0)),
                      pl.BlockSpec((1, 1, 2 * D_FF), lambda i, be, na: (be[i], 0, 0)),
                      pl.BlockSpec((1, D_FF, D_MODEL), lambda i, be, na: (be[i], 0, 0)),
                      pl.BlockSpec((1, 1, D_MODEL), lambda i, be, na: (be[i], 0, 0))],
            out_specs=pl.BlockSpec((MOE_BLK, D_MODEL // 2), lambda i, be, na: (i, 0))),
        out_shape=jax.ShapeDtypeStruct((n_slots, D_MODEL // 2), jnp.uint32),
        compiler_params=_cparams(("arbitrary",)),
        name="experts",
    )(block_e, n_active, xs, w1, b1, w2, b2)


def _combine_kernel(pstart_ref, cnt_ref, car_ref, h_ref, rloc_ref, gate_ref, gf_ref, outs_hbm, o_ref,
                    ybuf, total_s, sem_y):
    i = pl.program_id(0)
    nt = pl.num_programs(0)
    tc = h_ref.shape[0]
    slot = i % 2

    def fetch(tile, s):
        def piece(local, src, size):
            pltpu.make_async_copy(outs_hbm.at[pl.ds(src, size)], ybuf.at[s, pl.ds(local, size)],
                                  sem_y.at[s]).start()

        total_s[s] = _for_each_run(tile, cnt_ref, car_ref, pstart_ref, tc, piece)

    @pl.when(i == 0)
    def _():
        ybuf[...] = jnp.zeros_like(ybuf)
        fetch(i, slot)

    @pl.when(i + 1 < nt)
    def _():
        fetch(i + 1, 1 - slot)

    hits = _sorted_onehot(rloc_ref, tc)
    gsel = jnp.zeros((_sorted_rows(tc), tc), F32)
    for k in range(TOP_K):
        gsel = jnp.where(hits[k], gate_ref[k:k + 1, :], gsel)
    g_t = gsel.T
    g0 = g_t.astype(BF16)
    g1 = (g_t - g0.astype(F32)).astype(BF16)

    def wait_piece(done, size):
        del done
        pltpu.make_async_copy(outs_hbm.at[pl.ds(0, size)], ybuf.at[slot, pl.ds(0, size)],
                              sem_y.at[slot]).wait()

    _pow2_pieces(total_s[slot], TOP_K * tc, wait_piece)
    yb = _unpack_halves(ybuf[slot])
    o_ref[...] = _rms(h_ref[...] + _dot(g0, yb) + _dot(g1, yb), gf_ref[...])


def _combine(pstart, cnt, car, h2d, rloc, gate, gf, outs, tc):
    t = h2d.shape[0]
    kt = pl.BlockSpec((TOP_K, tc), lambda i, *_: (0, i))
    return pl.pallas_call(
        _combine_kernel,
        grid_spec=pltpu.PrefetchScalarGridSpec(
            num_scalar_prefetch=3,
            grid=(t // tc,),
            in_specs=[pl.BlockSpec((tc, D_MODEL), lambda i, *_: (i, 0)), kt, kt,
                      pl.BlockSpec((1, D_MODEL), lambda i, *_: (0, 0)),
                      pl.BlockSpec(memory_space=pl.ANY)],
            out_specs=pl.BlockSpec((tc, D_MODEL), lambda i, *_: (i, 0)),
            scratch_shapes=[pltpu.VMEM((2, _sorted_rows(tc), D_MODEL // 2), jnp.uint32),
                            pltpu.SMEM((2,), jnp.int32),
                            pltpu.SemaphoreType.DMA((2,))]),
        out_shape=jax.ShapeDtypeStruct((t, D_MODEL), F32),
        compiler_params=_cparams(("arbitrary",)),
        name="combine",
    )(pstart, cnt, car, h2d, rloc, gate, gf, outs)


def _prep_weights(g_mix_norm, w_in, conv_w, conv_b, a_log_fwd, a_log_bwd, dt_bias_fwd, dt_bias_bwd,
                  d_skip, g_fourier_norm, g_ssd_norm, w_out, g_xattn_norm, g_mem_norm, w_q, w_k,
                  w_v, w_o, g_moe_norm, w_router, b_router, w_mlp1, b_mlp1, w_mlp2, b_mlp2, g_final):
    s1 = FOURIER_W
    s2 = s1 + SSD_W
    s3 = s2 + CONV_CH
    w = w_in[0]
    wdt = jnp.zeros((D_MODEL, 2 * LANES), F32)
    wdt = wdt.at[:, :SSD_HEADS].set(w[:, s3:s3 + SSD_HEADS])
    wdt = wdt.at[:, LANES:LANES + SSD_HEADS].set(w[:, s3 + SSD_HEADS:])
    pad_heads = lambda a, b: jnp.zeros((2, 1, LANES), F32).at[0, 0, :SSD_HEADS].set(a[0]).at[
        1, 0, :SSD_HEADS].set(b[0])
    head_major = lambda a: jnp.broadcast_to(jnp.swapaxes(a, 1, 2), (2, LANES, LANES))
    alog = pad_heads(a_log_fwd, a_log_bwd)
    dtb = pad_heads(dt_bias_fwd, dt_bias_bwd)
    dsk = jnp.zeros((2, 1, SSD_W), F32).at[0, 0].set(jnp.repeat(d_skip[0], SSD_HEAD_DIM))
    lower = np.tril(np.ones((CHUNK, CHUNK), np.float32))
    return dict(
        g_mix=g_mix_norm[0][None], wu=w[:, :s1].astype(BF16), wz=w[:, s1:s2].astype(BF16),
        wx=w[:, s2:s3].astype(BF16), wdt=wdt.astype(BF16), wdtt=wdt.T.astype(BF16),
        cs=_channel_dft_table(), conv_w=conv_w[0, :, 0, :], conv_b=conv_b[0][None],
        tri=jnp.asarray(np.stack([lower, lower.T])), alog=alog, alogt=head_major(alog),
        dtb=dtb, dtbt=head_major(dtb), dsk=dsk,
        g_four=g_fourier_norm[0][None], g_ssd=g_ssd_norm[0][None],
        w4=w_out[0, :FOURIER_W].astype(BF16), wy=w_out[0, FOURIER_W:].astype(BF16),
        g_x=g_xattn_norm[0][None], g_mem=g_mem_norm[0][None],
        wq=w_q[0].astype(BF16), wk=w_k[0].astype(BF16), wv=w_v[0].astype(BF16), wo=w_o[0].astype(BF16),
        g_moe=g_moe_norm[0][None], wr_t=w_router[0].T, br=b_router[0][:, None],
        w1=w_mlp1[0].astype(BF16), b1=b_mlp1[0][:, None, :], w2=w_mlp2[0].astype(BF16),
        b2=b_mlp2[0][:, None, :], g_final=g_final[None])


def _encode_group(x, mem, p):
    bsz, seqlen, _ = x.shape
    t = bsz * seqlen
    tm = min(TOKEN_TILE, seqlen // 2)
    assert seqlen % (2 * tm) == 0 and tm % MOE_TILE == 0 and seqlen % (SSD_SUB * CHUNK) == 0, x.shape

    ab, z, xbc, dt, dtt = _inproj(x, p["g_mix"], p["wu"], p["wz"], p["wx"], p["wdt"], p["wdtt"],
                                  p["cs"], p["conv_w"], p["conv_b"], tm)
    cos_t, nsin_t, cmid_t = _dft_tables(seqlen)
    nb = 4 if bsz % 4 == 0 else (2 if bsz % 2 == 0 else 1)
    four = _fourier(ab.reshape(bsz, seqlen, 2 * FOURIER_W), cos_t, nsin_t, cmid_t, p["g_four"], nb,
                    tm, min(DFT_TILE, seqlen))
    yf, yb = _ssd(xbc.reshape(bsz, seqlen, CONV_CH), dt.reshape(bsz, seqlen, 2 * LANES), dtt,
                  p["tri"], p["alog"], p["alogt"], p["dtb"], p["dtbt"], p["dsk"])

    k, v = _kv(mem, p["g_mem"], p["wk"], p["wv"])
    h, xn, rloc, gate, cnt, car = _post(x, four, yf, yb, z.reshape(bsz, seqlen, SSD_W), k, v, p, tm)
    h = h.reshape(t, D_MODEL)
    xn = xn.reshape(t, D_MODEL)

    td = MOE_TILE
    counts = jnp.sum(cnt, axis=1)
    padded = ((counts + MOE_BLK - 1) // MOE_BLK) * MOE_BLK
    pend = jnp.cumsum(padded)
    pstart = (pend - padded).astype(jnp.int32)
    max_rows = t * TOP_K + (t // td) * N_EXPERTS * (RUN_ALIGN - 1)
    n_blocks = -(-max_rows // MOE_BLK) + N_EXPERTS
    n_slots = n_blocks * MOE_BLK
    n_active = (pend[-1] // MOE_BLK).astype(jnp.int32)
    blk = jnp.arange(n_blocks, dtype=jnp.int32)
    be = jnp.sum(pend[None, :] <= (jnp.minimum(blk, n_active - 1) * MOE_BLK)[:, None], axis=1)
    be = jnp.minimum(be, N_EXPERTS - 1).astype(jnp.int32)
    zstart = (pend - MOE_BLK).astype(jnp.int32)[jnp.argsort(padded == 0)]
    nzero = jnp.stack([jnp.sum(padded > 0).astype(jnp.int32), n_active])

    cnt_flat = cnt.T.reshape(-1)
    car_flat = car.T.reshape(-1)
    xs = _dispatch(pstart, cnt_flat, car_flat, zstart, nzero, xn, rloc, n_slots, td)
    outs = _experts(be, n_active[None], xs, p["w1"], p["b1"], p["w2"], p["b2"])
    y = _combine(pstart, cnt_flat, car_flat, h, rloc, gate, p["g_final"], outs, td)
    return y.reshape(bsz, seqlen, D_MODEL)


def kernel(x_prompt, x_sample, mem_prompt, mem_sample, g_mix_norm, w_in, conv_w, conv_b, a_log_fwd, a_log_bwd, dt_bias_fwd, dt_bias_bwd, d_skip, g_fourier_norm, g_ssd_norm, w_out, g_xattn_norm, g_mem_norm, w_q, w_k, w_v, w_o, g_moe_norm, w_router, b_router, w_mlp1, b_mlp1, w_mlp2, b_mlp2, g_final):
    p = _prep_weights(g_mix_norm, w_in, conv_w, conv_b, a_log_fwd, a_log_bwd, dt_bias_fwd,
                      dt_bias_bwd, d_skip, g_fourier_norm, g_ssd_norm, w_out, g_xattn_norm,
                      g_mem_norm, w_q, w_k, w_v, w_o, g_moe_norm, w_router, b_router, w_mlp1,
                      b_mlp1, w_mlp2, b_mlp2, g_final)
    return (_encode_group(x_prompt, mem_prompt, p), _encode_group(x_sample, mem_sample, p))
```

```python
import functools

import numpy as np
import jax
import jax.numpy as jnp
from jax import lax
from jax.experimental import pallas as pl
from jax.experimental.pallas import tpu as pltpu

F32 = jnp.float32
BF16 = jnp.bfloat16

D_MODEL = 1024
D_MIX = 2 * D_MODEL
FOURIER_W = D_MIX // 4
FOURIER_GROUPS = 8
FOURIER_GROUP_DIM = FOURIER_W // FOURIER_GROUPS
SSD_W = D_MIX - FOURIER_W
SSD_HEAD_DIM = 64
SSD_HEADS = SSD_W // SSD_HEAD_DIM
SSD_GROUPS = 4
SSD_HPG = SSD_HEADS // SSD_GROUPS
D_STATE = 128
CONV_WIDTH = 5
CONV_PAD = CONV_WIDTH // 2
GN = SSD_GROUPS * D_STATE
CONV_CH = SSD_W + 2 * GN
CHUNK = 128
SSD_SUB = 4
XATTN_HEADS = 4
XATTN_HEAD_DIM = D_MODEL // XATTN_HEADS
N_EXPERTS = 32
TOP_K = 4
D_FF = D_MODEL
SWIGLU_ALPHA = 1.702
SWIGLU_LIMIT = 7.0
NORM_EPS = 1e-5
LOG2E = 1.4426950408889634

LANES = 128
HALO = 8
CONV_COLS = 512
TOKEN_TILE = 512
DFT_TILE = 1024
CARRY_ROWS = 16
MOE_BLK = 512
MOE_TILE = 256
RUN_ALIGN = 8
RARE_RUN = 64
VMEM_LIMIT = 48 * 1024 * 1024


def _cparams(sem):
    return pltpu.CompilerParams(dimension_semantics=sem, vmem_limit_bytes=VMEM_LIMIT)


def _rms(x, g):
    return x * lax.rsqrt(jnp.mean(x * x, axis=-1, keepdims=True) + NORM_EPS) * g


def _sigmoid(x):
    return 1.0 / (1.0 + jnp.exp(-x))


def _softplus(x):
    return jnp.maximum(x, 0.0) + jnp.log1p(jnp.exp(-jnp.abs(x)))


def _dot(a, b):
    return jnp.dot(a, b, preferred_element_type=F32)


def _dot_nt(a, b):
    return lax.dot_general(a, b, (((1,), (1,)), ((), ())), preferred_element_type=F32)


def _pack_halves(x):
    c = x.shape[1] // 2
    u = lax.bitcast_convert_type(x, jnp.uint32)
    return (u[:, :c] >> 16) | (u[:, c:] & jnp.uint32(0xFFFF0000))


def _unpack_halves(w):
    lo = lax.bitcast_convert_type(w << 16, F32)
    hi = lax.bitcast_convert_type(w & jnp.uint32(0xFFFF0000), F32)
    return jnp.concatenate([lo, hi], axis=1).astype(BF16)


def _split3(x):
    h = x.astype(BF16)
    r = x - h.astype(F32)
    m = r.astype(BF16)
    l = (r - m.astype(F32)).astype(BF16)
    return h, m, l


def _inproj_kernel(prev_ref, x_ref, next_ref, g_ref, wu_ref, wz_ref, wx_ref, wdt_ref, wdtt_ref,
                   cs_ref, cw_ref, cb_ref, ab_ref, z_ref, xbc_ref, dt_ref, dtt_ref):
    i = pl.program_id(1)
    tm = x_ref.shape[1]
    rows = tm + 2 * HALO
    mid = slice(HALO, HALO + tm)
    xa = jnp.concatenate([prev_ref[0], x_ref[0], next_ref[0]], axis=0)
    xn = _rms(xa, g_ref[...]).astype(BF16)
    u = _dot(xn, wu_ref[...])[mid]
    ab_ref[...] = _dot(u.astype(BF16), cs_ref[...]).astype(BF16)
    z_ref[...] = _dot(xn, wz_ref[...])[mid].astype(BF16)
    dt_ref[...] = _dot(xn, wdt_ref[...])[mid]
    dtt_ref[...] = _dot_nt(wdtt_ref[...], xn)[:, mid]

    r = lax.broadcasted_iota(jnp.int32, (rows, 1), 0)
    has_prev = jnp.where(i > 0, 1.0, 0.0)
    has_next = jnp.where(i < pl.num_programs(1) - 1, 1.0, 0.0)
    inside = jnp.where(r < HALO, has_prev, jnp.where(r >= HALO + tm, has_next, 1.0))
    for c in range(CONV_CH // CONV_COLS):
        cols = slice(c * CONV_COLS, (c + 1) * CONV_COLS)
        xe = _dot(xn, wx_ref[:, cols]) * inside
        acc = cb_ref[:, cols] + cw_ref[CONV_PAD:CONV_PAD + 1, cols] * xe
        for w in range(CONV_WIDTH):
            if w != CONV_PAD:
                acc = acc + cw_ref[w:w + 1, cols] * pltpu.roll(xe, (CONV_PAD - w) % rows, axis=0)
        y = acc[mid]
        xbc_ref[:, cols] = (y * _sigmoid(y)).astype(BF16)


def _inproj(x, g, wu, wz, wx, wdt, wdtt, cs, cw, cb, tm):
    bsz, seqlen, _ = x.shape
    t = bsz * seqlen
    nt = seqlen // tm
    r = tm // HALO
    nh = seqlen // HALO
    full = lambda a: pl.BlockSpec(a.shape, lambda b, i: (0,) * a.ndim)
    row = lambda w: pl.BlockSpec((tm, w), lambda b, i: (b * nt + i, 0))
    return pl.pallas_call(
        _inproj_kernel,
        grid=(bsz, nt),
        in_specs=[pl.BlockSpec((1, HALO, D_MODEL), lambda b, i: (b, jnp.maximum(i * r - 1, 0), 0)),
                  pl.BlockSpec((1, tm, D_MODEL), lambda b, i: (b, i, 0)),
                  pl.BlockSpec((1, HALO, D_MODEL), lambda b, i: (b, jnp.minimum((i + 1) * r, nh - 1), 0)),
                  full(g), full(wu), full(wz), full(wx), full(wdt), full(wdtt), full(cs), full(cw),
                  full(cb)],
        out_specs=[row(2 * FOURIER_W), row(SSD_W), row(CONV_CH), row(2 * LANES),
                   pl.BlockSpec((2 * LANES, tm), lambda b, i: (0, b * nt + i))],
        out_shape=[jax.ShapeDtypeStruct((t, 2 * FOURIER_W), BF16),
                   jax.ShapeDtypeStruct((t, SSD_W), BF16),
                   jax.ShapeDtypeStruct((t, CONV_CH), BF16),
                   jax.ShapeDtypeStruct((t, 2 * LANES), F32),
                   jax.ShapeDtypeStruct((2 * LANES, t), F32)],
        compiler_params=_cparams(("parallel", "parallel")),
        name="inproj",
    )(x, x, x, g, wu, wz, wx, wdt, wdtt, cs, cw, cb)


def _fourier_kernel(c_ref, s_ref, cm_ref, ab_ref, g_ref, o_ref, accp_ref, accq_ref, accm_ref,
                    carry_ref, *, scale, nb):
    j = pl.program_id(1)
    k = pl.program_id(2)
    tr = c_ref.shape[0]

    @pl.when(k == 0)
    def _():
        accp_ref[...] = jnp.zeros_like(accp_ref)
        accq_ref[...] = jnp.zeros_like(accq_ref)

    @pl.when((k == 0) & (j == 0))
    def _():
        accm_ref[...] = jnp.zeros_like(accm_ref)

    c = c_ref[...]
    s = s_ref[...]
    for b in range(nb):
        accp_ref[b] += _dot(c, ab_ref[b, :, :FOURIER_W])
        accq_ref[b] += _dot(s, ab_ref[b, :, FOURIER_W:])

    @pl.when(j == 0)
    def _():
        for b in range(nb):
            accm_ref[b] += _dot(cm_ref[...], ab_ref[b, :, :FOURIER_W])

    @pl.when(k == pl.num_programs(2) - 1)
    def _():
        g = g_ref[...]
        r = lax.broadcasted_iota(jnp.int32, (tr, tr + CARRY_ROWS), 0)
        col = lax.broadcasted_iota(jnp.int32, (tr, tr + CARRY_ROWS), 1)
        flip = jnp.where(col == jnp.where(r == 0, tr, tr - r), 1.0, 0.0).astype(BF16)
        for b in range(nb):
            @pl.when(j == 0)
            def _():
                carry_ref[b] = _rms(accm_ref[b] * scale, g).astype(BF16)

            p = accp_ref[b] * scale
            q = accq_ref[b] * scale
            o_ref[b, 0] = _rms(p + q, g).astype(BF16)
            m = _rms(p - q, g).astype(BF16)
            o_ref[b, 1] = _dot(flip, jnp.concatenate([m, carry_ref[b]], axis=0)).astype(BF16)
            carry_ref[b] = m[:CARRY_ROWS]


def _fourier(ab, cos_t, nsin_t, cmid_t, g, nb, tr, tk):
    bsz, seqlen, _ = ab.shape
    half_tiles = seqlen // (2 * tr)
    scale = float(1.0 / np.sqrt(seqlen * FOURIER_GROUP_DIM))
    tile = lambda b, j, k: (half_tiles - 1 - j, k)
    return pl.pallas_call(
        functools.partial(_fourier_kernel, scale=scale, nb=nb),
        grid=(bsz // nb, half_tiles, seqlen // tk),
        in_specs=[pl.BlockSpec((tr, tk), tile),
                  pl.BlockSpec((tr, tk), tile),
                  pl.BlockSpec((CARRY_ROWS, tk), lambda b, j, k: (0, k)),
                  pl.BlockSpec((nb, tk, 2 * FOURIER_W), lambda b, j, k: (b, k, 0)),
                  pl.BlockSpec((1, FOURIER_W), lambda b, j, k: (0, 0))],
        out_specs=pl.BlockSpec((nb, 2, tr, FOURIER_W), lambda b, j, k: (b, 0, half_tiles - 1 - j, 0)),
        out_shape=jax.ShapeDtypeStruct((bsz, 2, seqlen // 2, FOURIER_W), BF16),
        scratch_shapes=[pltpu.VMEM((nb, tr, FOURIER_W), F32),
                        pltpu.VMEM((nb, tr, FOURIER_W), F32),
                        pltpu.VMEM((nb, CARRY_ROWS, FOURIER_W), F32),
                        pltpu.VMEM((nb, CARRY_ROWS, FOURIER_W), BF16)],
        compiler_params=_cparams(("parallel", "arbitrary", "arbitrary")),
        name="fourier",
    )(cos_t, nsin_t, cmid_t, ab, g)


def _dft_tables(seqlen):
    lo = min(seqlen // 2, LANES)
    hi = seqlen // 2 // lo
    l = np.arange(seqlen, dtype=np.int64)
    a_hi = 2.0 * np.pi * ((np.arange(hi, dtype=np.int64)[:, None] * lo * l[None, :]) % seqlen) / seqlen
    a_lo = 2.0 * np.pi * ((np.arange(lo, dtype=np.int64)[:, None] * l[None, :]) % seqlen) / seqlen
    c1 = jnp.asarray(np.cos(a_hi), F32)[:, None, :]
    s1 = jnp.asarray(np.sin(a_hi), F32)[:, None, :]
    c0 = jnp.asarray(np.cos(a_lo), F32)[None, :, :]
    s0 = jnp.asarray(np.sin(a_lo), F32)[None, :, :]
    cos_t = (c1 * c0 - s1 * s0).reshape(seqlen // 2, seqlen).astype(BF16)
    nsin_t = (-(s1 * c0 + c1 * s0)).reshape(seqlen // 2, seqlen).astype(BF16)
    cmid = np.zeros((CARRY_ROWS, seqlen), np.float32)
    cmid[0] = 1.0 - 2.0 * (l % 2)
    return cos_t, nsin_t, jnp.asarray(cmid, BF16)


def _channel_dft_table():
    c = np.arange(FOURIER_W)
    same = (c[:, None] // FOURIER_GROUP_DIM) == (c[None, :] // FOURIER_GROUP_DIM)
    ang = 2.0 * np.pi * ((c[:, None] % FOURIER_GROUP_DIM) * (c[None, :] % FOURIER_GROUP_DIM)
                         % FOURIER_GROUP_DIM) / FOURIER_GROUP_DIM
    cc = np.where(same, np.cos(ang), 0.0)
    sc = np.where(same, np.sin(ang), 0.0)
    return jnp.asarray(np.concatenate([cc, sc], axis=1), BF16)


def _ssd_kernel(xf_ref, xb_ref, dtf_ref, dtb_ref, dttf_ref, dttb_ref, tri_ref, alog_ref, alogt_ref,
                bias_ref, biast_ref, dsk_ref, yf_ref, yb_ref, state_ref):
    @pl.when(pl.program_id(1) == 0)
    def _():
        state_ref[...] = jnp.zeros_like(state_ref)

    for c in range(SSD_SUB):
        for d, (x_ref, dt_ref, dtt_ref, y_ref) in enumerate(((xf_ref, dtf_ref, dttf_ref, yf_ref),
                                                             (xb_ref, dtb_ref, dttb_ref, yb_ref))):
            sub = c if d == 0 else SSD_SUB - 1 - c
            _ssd_chunk(slice(sub * CHUNK, (sub + 1) * CHUNK), x_ref, dt_ref, dtt_ref, tri_ref.at[d],
                       tri_ref.at[1 - d], alog_ref.at[d], alogt_ref.at[d], bias_ref.at[d],
                       biast_ref.at[d], dsk_ref.at[d], y_ref, state_ref.at[d])


def _ssd_chunk(rows, xbc_ref, dt_ref, dtt_ref, tri_ref, trit_ref, alog_ref, alogt_ref, dtb_ref,
               dtbt_ref, dsk_ref, y_ref, state_ref):
    q = CHUNK
    tri = tri_ref[...]
    mask = tri > 0.5
    dt = _softplus(dt_ref[0, rows, :] + dtb_ref[...])
    d0, d1, d2 = _split3(dt * -jnp.exp(alog_ref[...]))
    tri_b = tri.astype(BF16)
    acum2 = (_dot(tri_b, d0) + _dot(tri_b, d1) + _dot(tri_b, d2)) * LOG2E
    dt_t = _softplus(dtt_ref[:, rows] + dtbt_ref[...])
    t0, t1, t2 = _split3(dt_t * -jnp.exp(alogt_ref[...]))
    trit_b = trit_ref[...].astype(BF16)
    ones_b = jnp.ones((q, LANES), BF16)
    acum_t2 = (_dot(t0, trit_b) + _dot(t1, trit_b) + _dot(t2, trit_b)) * LOG2E
    tot2 = (_dot(t0, ones_b) + _dot(t1, ones_b) + _dot(t2, ones_b)) * LOG2E
    row2 = acum_t2 - jnp.log2(dt_t)
    w_t = dt_t * jnp.exp2(tot2 - acum_t2)
    etot = jnp.exp2(tot2)
    lane_lo = lax.broadcasted_iota(jnp.int32, (1, LANES), 1) < SSD_HEAD_DIM

    for g in range(SSD_GROUPS):
        bm = xbc_ref[0, rows, SSD_W + g * D_STATE:SSD_W + (g + 1) * D_STATE]
        cm = xbc_ref[0, rows, SSD_W + GN + g * D_STATE:SSD_W + GN + (g + 1) * D_STATE]
        cb = _dot_nt(cm, bm)
        cm_f = cm.astype(F32)
        bm_t = bm.astype(F32).T
        for p in range(SSD_HPG // 2):
            t = g * (SSD_HPG // 2) + p
            cols = slice(t * LANES, (t + 1) * LANES)
            xs = xbc_ref[0, rows, cols]
            st = state_ref[:, cols]
            lhs = []
            lhs2 = []
            for h in (2 * t, 2 * t + 1):
                colb2 = jnp.broadcast_to(acum2[:, h:h + 1], (q, q))
                decay_dt = jnp.where(mask, jnp.exp2(colb2 - row2[h:h + 1, :]), 0.0)
                lhs.append(jnp.concatenate([(cb * decay_dt).astype(BF16),
                                            (cm_f * jnp.exp2(colb2)).astype(BF16)], axis=1))
                lhs2.append((bm_t * w_t[h:h + 1, :]).astype(BF16))
            yy = _dot(jnp.concatenate(lhs, axis=0), jnp.concatenate([xs, st.astype(BF16)], axis=0))
            y = jnp.where(lane_lo, yy[:q], yy[q:]) + dsk_ref[:, cols] * xs.astype(F32)
            y_ref[0, rows, cols] = y.astype(BF16)
            dec = jnp.where(lane_lo, etot[2 * t:2 * t + 1, :], etot[2 * t + 1:2 * t + 2, :])
            ss = _dot(jnp.concatenate(lhs2, axis=0), xs)
            state_ref[:, cols] = st * dec + jnp.where(lane_lo, ss[:D_STATE], ss[D_STATE:])


def _ssd(xbc, dt, dtt, tri, alog, alogt, dtb, dtbt, dsk):
    bsz, seqlen, _ = xbc.shape
    rows = SSD_SUB * CHUNK
    nc = seqlen // rows
    full = lambda a: pl.BlockSpec(a.shape, lambda b, j: (0,) * a.ndim)
    return pl.pallas_call(
        _ssd_kernel,
        grid=(bsz, nc),
        in_specs=[pl.BlockSpec((1, rows, CONV_CH), lambda b, j: (b, j, 0)),
                  pl.BlockSpec((1, rows, CONV_CH), lambda b, j: (b, nc - 1 - j, 0)),
                  pl.BlockSpec((1, rows, LANES), lambda b, j: (b, j, 0)),
                  pl.BlockSpec((1, rows, LANES), lambda b, j: (b, nc - 1 - j, 1)),
                  pl.BlockSpec((LANES, rows), lambda b, j: (0, b * nc + j)),
                  pl.BlockSpec((LANES, rows), lambda b, j: (1, b * nc + nc - 1 - j)),
                  full(tri), full(alog), full(alogt), full(dtb), full(dtbt), full(dsk)],
        out_specs=[pl.BlockSpec((1, rows, SSD_W), lambda b, j: (b, j, 0)),
                   pl.BlockSpec((1, rows, SSD_W), lambda b, j: (b, nc - 1 - j, 0))],
        out_shape=[jax.ShapeDtypeStruct((bsz, seqlen, SSD_W), BF16)] * 2,
        scratch_shapes=[pltpu.VMEM((2, D_STATE, SSD_W), F32)],
        compiler_params=_cparams(("parallel", "arbitrary")),
        name="ssd",
    )(xbc, xbc, dt, dt, dtt, dtt, tri, alog, alogt, dtb, dtbt, dsk)


def _kv_kernel(mem_ref, g_ref, wk_ref, wv_ref, k_ref, v_ref):
    mn = _rms(mem_ref[0], g_ref[...]).astype(BF16)
    k_ref[0] = _dot(mn, wk_ref[...]).astype(BF16)
    v_ref[0] = _dot(mn, wv_ref[...]).astype(BF16)


def _kv(mem, g, wk, wv):
    bsz, nm, _ = mem.shape
    full = lambda a: pl.BlockSpec(a.shape, lambda b: (0,) * a.ndim)
    blk = pl.BlockSpec((1, nm, D_MODEL), lambda b: (b, 0, 0))
    return pl.pallas_call(
        _kv_kernel,
        grid=(bsz,),
        in_specs=[blk, full(g), full(wk), full(wv)],
        out_specs=[blk, blk],
        out_shape=[jax.ShapeDtypeStruct(mem.shape, BF16)] * 2,
        compiler_params=_cparams(("parallel",)),
        name="kv",
    )(mem, g, wk, wv)


def _cross_attend(h, g, wq_ref, k_ref, v_ref, wo_ref):
    hn = _rms(h, g).astype(BF16)
    q = (_dot(hn, wq_ref[...]) * (XATTN_HEAD_DIM ** -0.5)).astype(BF16)
    outs = []
    for hd in range(XATTN_HEADS):
        cols = slice(hd * XATTN_HEAD_DIM, (hd + 1) * XATTN_HEAD_DIM)
        s = _dot_nt(q[:, cols], k_ref[0, :, cols])
        e = jnp.exp(s - jnp.max(s, axis=-1, keepdims=True))
        p = (e / jnp.sum(e, axis=-1, keepdims=True)).astype(BF16)
        outs.append(_dot(p, v_ref[0, :, cols]).astype(BF16))
    return h + _dot(jnp.concatenate(outs, axis=1), wo_ref[...])


def _route_tile(xn, i, cols, w0, w1, br, rloc_ref, gate_ref, carry_ref, cnt_acc, car_acc):
    tm = xn.shape[0]
    x0, x1, _ = _split3(xn)
    logits = _dot_nt(w0, x0) + _dot_nt(w0, x1) + _dot_nt(w1, x0) + br

    eid = lax.broadcasted_iota(jnp.int32, (N_EXPERTS, tm), 0)
    work = logits
    vals, hits = [], []
    sel = jnp.zeros((N_EXPERTS, tm), F32)
    for k in range(TOP_K):
        m = jnp.max(work, axis=0, keepdims=True)
        ik = jnp.min(jnp.where(work == m, eid, N_EXPERTS), axis=0, keepdims=True)
        hit = eid == ik
        vals.append(m)
        hits.append(hit)
        sel = jnp.where(hit, 1.0, sel)
        work = jnp.where(hit, -jnp.inf, work)
    es = [jnp.exp(v - vals[0]) for v in vals]
    den = es[0] + es[1] + es[2] + es[3]
    for k in range(TOP_K):
        gate_ref[k:k + 1, cols] = es[k] / den

    r = lax.broadcasted_iota(jnp.int32, (tm, tm), 0)
    c = lax.broadcasted_iota(jnp.int32, (tm, tm), 1)
    before = jnp.where(r < c, 1.0, 0.0).astype(BF16)
    cum = _dot(sel.astype(BF16), before)
    counts = jnp.sum(sel, axis=1, keepdims=True)
    counts = jnp.floor((counts + (RUN_ALIGN - 1)) * (1.0 / RUN_ALIGN)) * RUN_ALIGN
    er = lax.broadcasted_iota(jnp.int32, (N_EXPERTS, N_EXPERTS), 0)
    ec = lax.broadcasted_iota(jnp.int32, (N_EXPERTS, N_EXPERTS), 1)
    lower = jnp.where(er > ec, 1.0, 0.0).astype(BF16)
    offs = _dot(lower, jnp.broadcast_to(counts, (N_EXPERTS, LANES)).astype(BF16))[:, 0:1]
    pos = cum + offs
    for k in range(TOP_K):
        rk = jnp.sum(jnp.where(hits[k], pos, 0.0), axis=0, keepdims=True)
        rloc_ref[k:k + 1, cols] = rk.astype(jnp.int32)

    lane = lax.broadcasted_iota(jnp.int32, cnt_acc.shape, 1)
    cnt_acc[...] = jnp.where(lane == i, counts, cnt_acc[...])
    car_acc[...] = jnp.where(lane == i, carry_ref[:, 0:1], car_acc[...])
    carry_ref[...] = carry_ref[...] + counts


def _post_kernel(x_ref, four_ref, yf_ref, yb_ref, z_ref, gs_ref, w4_ref, wy_ref, gx_ref, wq_ref, k_ref, v_ref,
                 wo_ref, gm_ref, wr_ref, br_ref, h_ref, xn_ref, rloc_ref, gate_ref, cnt_ref, car_ref,
                 carry_ref, cnt_acc, car_acc):
    step = pl.program_id(0) * pl.num_programs(1) + pl.program_id(1)
    tm = x_ref.shape[1]

    @pl.when(step == 0)
    def _():
        carry_ref[...] = jnp.zeros_like(carry_ref)
        cnt_acc[...] = jnp.zeros_like(cnt_acc)
        car_acc[...] = jnp.zeros_like(car_acc)

    y = yf_ref[0].astype(F32) + yb_ref[0].astype(F32)
    z = z_ref[0].astype(F32)
    yn = _rms(y * (z * _sigmoid(z)), gs_ref[...]).astype(BF16)
    h = x_ref[0] + _dot(four_ref[0, 0], w4_ref[...]) + _dot(yn, wy_ref[...])

    h = _cross_attend(h, gx_ref[...], wq_ref, k_ref, v_ref, wo_ref)
    h_ref[0] = h

    xn = _rms(h, gm_ref[...])
    xn_ref[0] = xn.astype(BF16)
    w0, w1, _ = _split3(wr_ref[...])
    sub = tm // MOE_TILE
    for s in range(sub):
        rows = slice(s * MOE_TILE, (s + 1) * MOE_TILE)
        _route_tile(xn[rows], step * sub + s, rows, w0, w1, br_ref[...], rloc_ref, gate_ref,
                    carry_ref, cnt_acc, car_acc)
    cnt_ref[...] = cnt_acc[...].astype(jnp.int32)
    car_ref[...] = car_acc[...].astype(jnp.int32)


def _post(x, four, yf, yb, z, mem_k, mem_v, p, tm):
    bsz, seqlen, _ = x.shape
    t = bsz * seqlen
    nt = seqlen // tm
    n_tiles = t // MOE_TILE
    nm = mem_k.shape[1]
    weights = [p["g_ssd"], p["w4"], p["wy"], p["g_x"], p["wq"]]
    weights2 = [p["wo"], p["g_moe"], p["wr_t"], p["br"]]
    full = lambda a: pl.BlockSpec(a.shape, lambda b, i: (0,) * a.ndim)
    tok = lambda w: pl.BlockSpec((1, tm, w), lambda b, i: (b, i, 0))
    mem = pl.BlockSpec((1, nm, D_MODEL), lambda b, i: (b, 0, 0))
    kt = pl.BlockSpec((TOP_K, tm), lambda b, i: (0, b * nt + i))
    per_tile = pl.BlockSpec((N_EXPERTS, n_tiles), lambda b, i: (0, 0))
    return pl.pallas_call(
        _post_kernel,
        grid=(bsz, nt),
        in_specs=[tok(D_MODEL),
                  pl.BlockSpec((1, 1, tm, FOURIER_W), lambda b, i: (
                      b, i // (nt // 2), jnp.where(i < nt // 2, i, nt - 1 - i), 0)),
                  tok(SSD_W), tok(SSD_W), tok(SSD_W)]
                 + [full(a) for a in weights] + [mem, mem] + [full(a) for a in weights2],
        out_specs=[tok(D_MODEL), tok(D_MODEL), kt, kt, per_tile, per_tile],
        out_shape=[jax.ShapeDtypeStruct((bsz, seqlen, D_MODEL), F32),
                   jax.ShapeDtypeStruct((bsz, seqlen, D_MODEL), BF16),
                   jax.ShapeDtypeStruct((TOP_K, t), jnp.int32),
                   jax.ShapeDtypeStruct((TOP_K, t), F32),
                   jax.ShapeDtypeStruct((N_EXPERTS, n_tiles), jnp.int32),
                   jax.ShapeDtypeStruct((N_EXPERTS, n_tiles), jnp.int32)],
        scratch_shapes=[pltpu.VMEM((N_EXPERTS, LANES), F32),
                        pltpu.VMEM((N_EXPERTS, n_tiles), F32),
                        pltpu.VMEM((N_EXPERTS, n_tiles), F32)],
        compiler_params=_cparams(("arbitrary", "arbitrary")),
        name="post",
    )(x, four, yf, yb, z, *weights, mem_k, mem_v, *weights2)


def _pow2_pieces(n, max_size, piece_fn, rare_size=None):
    def pieces(sizes, done):
        for size in sizes:
            bit = n & size

            @pl.when(bit != 0)
            def _(size=size, done=done):
                piece_fn(done, size)

            done = done + bit

    sizes = []
    size = max_size
    while size >= RUN_ALIGN:
        sizes.append(size)
        size //= 2
    rare = [s for s in sizes if rare_size is not None and s >= rare_size]
    if rare:
        pl.when(n >= rare_size)(lambda: pieces(rare, jnp.int32(0)))
    pieces(sizes[len(rare):], n & -rare[-1] if rare else jnp.int32(0))


def _for_each_run(tile, cnt_ref, car_ref, pstart_ref, tile_rows, piece_fn):
    def per_expert(e, local):
        n = cnt_ref[tile * N_EXPERTS + e]
        local = pl.multiple_of(local, RUN_ALIGN)
        slot = pl.multiple_of(pstart_ref[e] + car_ref[tile * N_EXPERTS + e], RUN_ALIGN)
        _pow2_pieces(n, tile_rows, lambda done, size: piece_fn(
            pl.multiple_of(local + done, RUN_ALIGN), pl.multiple_of(slot + done, RUN_ALIGN), size),
            rare_size=RARE_RUN)
        return local + n

    return lax.fori_loop(0, N_EXPERTS, per_expert, jnp.int32(0))


def _sorted_rows(tile_rows):
    return TOP_K * tile_rows + N_EXPERTS * RUN_ALIGN


def _sorted_onehot(rloc_ref, tile_rows):
    rows = lax.broadcasted_iota(jnp.int32, (_sorted_rows(tile_rows), tile_rows), 0)
    return [rows == rloc_ref[k:k + 1, :] for k in range(TOP_K)]


def _dispatch_kernel(pstart_ref, cnt_ref, car_ref, zstart_ref, nzero_ref, x_ref, rloc_ref, xs_hbm,
                     ybuf, zbuf, total_s, sem_z, sem_y):
    i = pl.program_id(0)
    td = x_ref.shape[0]
    slot = i % 2

    def wait_slot(s):
        def wait_piece(done, size):
            del done
            pltpu.make_async_copy(ybuf.at[s, pl.ds(0, size)], xs_hbm.at[pl.ds(0, size)],
                                  sem_y.at[s]).wait()

        _pow2_pieces(total_s[s], TOP_K * td, wait_piece)

    @pl.when(i == 0)
    def _():
        zbuf[...] = jnp.zeros_like(zbuf)

        def zcopy(start):
            dst = pl.ds(pl.multiple_of(start, MOE_BLK), MOE_BLK)
            return pltpu.make_async_copy(zbuf, xs_hbm.at[dst], sem_z)

        def issue(j, carry):
            zcopy(zstart_ref[j]).start()
            return carry

        def issue_tail(j, carry):
            zcopy(j * MOE_BLK).start()
            return carry

        def drain(j, carry):
            zcopy(0).wait()
            return carry

        n_blocks = xs_hbm.shape[0] // MOE_BLK
        lax.fori_loop(0, nzero_ref[0], issue, 0)
        lax.fori_loop(nzero_ref[1], n_blocks, issue_tail, 0)
        lax.fori_loop(0, nzero_ref[0] + (n_blocks - nzero_ref[1]), drain, 0)

    @pl.when(i >= 2)
    def _():
        wait_slot(slot)

    onehot = jnp.zeros((_sorted_rows(td), td), F32)
    for hit in _sorted_onehot(rloc_ref, td):
        onehot = jnp.where(hit, 1.0, onehot)
    ybuf[slot] = _pack_halves(_dot(onehot.astype(BF16), x_ref[...]))

    def send(local, dst, size):
        pltpu.make_async_copy(ybuf.at[slot, pl.ds(local, size)], xs_hbm.at[pl.ds(dst, size)],
                              sem_y.at[slot]).start()

    total_s[slot] = _for_each_run(i, cnt_ref, car_ref, pstart_ref, td, send)

    @pl.when(i == pl.num_programs(0) - 1)
    def _():
        wait_slot(slot)

        @pl.when(i >= 1)
        def _():
            wait_slot(1 - slot)


def _dispatch(pstart, cnt, car, zstart, nzero, xn, rloc, n_slots, td):
    t = xn.shape[0]
    half = D_MODEL // 2
    return pl.pallas_call(
        _dispatch_kernel,
        grid_spec=pltpu.PrefetchScalarGridSpec(
            num_scalar_prefetch=5,
            grid=(t // td,),
            in_specs=[pl.BlockSpec((td, D_MODEL), lambda i, *_: (i, 0)),
                      pl.BlockSpec((TOP_K, td), lambda i, *_: (0, i))],
            out_specs=pl.BlockSpec(memory_space=pl.ANY),
            scratch_shapes=[pltpu.VMEM((2, _sorted_rows(td), half), jnp.uint32),
                            pltpu.VMEM((MOE_BLK, half), jnp.uint32),
                            pltpu.SMEM((2,), jnp.int32),
                            pltpu.SemaphoreType.DMA,
                            pltpu.SemaphoreType.DMA((2,))]),
        out_shape=jax.ShapeDtypeStruct((n_slots, half), jnp.uint32),
        compiler_params=_cparams(("arbitrary",)),
        name="dispatch",
    )(pstart, cnt, car, zstart, nzero, xn, rloc)


def _expert_kernel(be_ref, na_ref, xs_ref, w1_ref, b1_ref, w2_ref, b2_ref, o_ref):
    i = pl.program_id(0)

    @pl.when(i < na_ref[0])
    def _():
        h = _dot(_unpack_halves(xs_ref[...]), w1_ref[0]) + b1_ref[0]
        glu = jnp.minimum(h[:, :D_FF], SWIGLU_LIMIT)
        lin = jnp.clip(h[:, D_FF:], -SWIGLU_LIMIT, SWIGLU_LIMIT)
        act = glu * _sigmoid(SWIGLU_ALPHA * glu) * (lin + 1.0)
        o = (_dot(act.astype(BF16), w2_ref[0]) + b2_ref[0]).astype(BF16).astype(F32)
        o_ref[...] = _pack_halves(o)

    @pl.when(i >= na_ref[0])
    def _():
        o_ref[...] = jnp.zeros_like(o_ref)


def _experts(block_e, n_active, xs, w1, b1, w2, b2):
    n_slots = xs.shape[0]
    return pl.pallas_call(
        _expert_kernel,
        grid_spec=pltpu.PrefetchScalarGridSpec(
            num_scalar_prefetch=2,
            grid=(n_slots // MOE_BLK,),
            in_specs=[pl.BlockSpec((MOE_BLK, D_MODEL // 2),
                                   lambda i, be, na: (jnp.minimum(i, na[0] - 1), 0)),
                      pl.BlockSpec((1, D_MODEL, 2 * D_FF), lambda i, be, na: (be[i], 0, 0)),
                      pl.BlockSpec((1, 1, 2 * D_FF), lambda i, be, na: (be[i], 0, 0)),
                      pl.BlockSpec((1, D_FF, D_MODEL), lambda i, be, na: (be[i], 0, 0)),
                      pl.BlockSpec((1, 1, D_MODEL), lambda i, be, na: (be[i], 0, 0))],
            out_specs=pl.BlockSpec((MOE_BLK, D_MODEL // 2), lambda i, be, na: (i, 0))),
        out_shape=jax.ShapeDtypeStruct((n_slots, D_MODEL // 2), jnp.uint32),
        compiler_params=_cparams(("arbitrary",)),
        name="experts",
    )(block_e, n_active, xs, w1, b1, w2, b2)


def _combine_kernel(pstart_ref, cnt_ref, car_ref, h_ref, rloc_ref, gate_ref, gf_ref, outs_hbm, o_ref,
                    ybuf, total_s, sem_y):
    i = pl.program_id(0)
    nt = pl.num_programs(0)
    tc = h_ref.shape[0]
    slot = i % 2

    def fetch(tile, s):
        def piece(local, src, size):
            pltpu.make_async_copy(outs_hbm.at[pl.ds(src, size)], ybuf.at[s, pl.ds(local, size)],
                                  sem_y.at[s]).start()

        total_s[s] = _for_each_run(tile, cnt_ref, car_ref, pstart_ref, tc, piece)

    @pl.when(i == 0)
    def _():
        ybuf[...] = jnp.zeros_like(ybuf)
        fetch(i, slot)

    @pl.when(i + 1 < nt)
    def _():
        fetch(i + 1, 1 - slot)

    hits = _sorted_onehot(rloc_ref, tc)
    gsel = jnp.zeros((_sorted_rows(tc), tc), F32)
    for k in range(TOP_K):
        gsel = jnp.where(hits[k], gate_ref[k:k + 1, :], gsel)
    g_b = gsel.T.astype(BF16)

    def wait_piece(done, size):
        del done
        pltpu.make_async_copy(outs_hbm.at[pl.ds(0, size)], ybuf.at[slot, pl.ds(0, size)],
                              sem_y.at[slot]).wait()

    _pow2_pieces(total_s[slot], TOP_K * tc, wait_piece)
    yb = _unpack_halves(ybuf[slot])
    o_ref[...] = _rms(h_ref[...] + _dot(g_b, yb), gf_ref[...])


def _combine(pstart, cnt, car, h2d, rloc, gate, gf, outs, tc):
    t = h2d.shape[0]
    kt = pl.BlockSpec((TOP_K, tc), lambda i, *_: (0, i))
    return pl.pallas_call(
        _combine_kernel,
        grid_spec=pltpu.PrefetchScalarGridSpec(
            num_scalar_prefetch=3,
            grid=(t // tc,),
            in_specs=[pl.BlockSpec((tc, D_MODEL), lambda i, *_: (i, 0)), kt, kt,
                      pl.BlockSpec((1, D_MODEL), lambda i, *_: (0, 0)),
                      pl.BlockSpec(memory_space=pl.ANY)],
            out_specs=pl.BlockSpec((tc, D_MODEL), lambda i, *_: (i, 0)),
            scratch_shapes=[pltpu.VMEM((2, _sorted_rows(tc), D_MODEL // 2), jnp.uint32),
                            pltpu.SMEM((2,), jnp.int32),
                            pltpu.SemaphoreType.DMA((2,))]),
        out_shape=jax.ShapeDtypeStruct((t, D_MODEL), F32),
        compiler_params=_cparams(("arbitrary",)),
        name="combine",
    )(pstart, cnt, car, h2d, rloc, gate, gf, outs)


def _prep_weights(g_mix_norm, w_in, conv_w, conv_b, a_log_fwd, a_log_bwd, dt_bias_fwd, dt_bias_bwd,
                  d_skip, g_fourier_norm, g_ssd_norm, w_out, g_xattn_norm, g_mem_norm, w_q, w_k,
                  w_v, w_o, g_moe_norm, w_router, b_router, w_mlp1, b_mlp1, w_mlp2, b_mlp2, g_final):
    s1 = FOURIER_W
    s2 = s1 + SSD_W
    s3 = s2 + CONV_CH
    w = w_in[0]
    wdt = jnp.zeros((D_MODEL, 2 * LANES), F32)
    wdt = wdt.at[:, :SSD_HEADS].set(w[:, s3:s3 + SSD_HEADS])
    wdt = wdt.at[:, LANES:LANES + SSD_HEADS].set(w[:, s3 + SSD_HEADS:])
    pad_heads = lambda a, b: jnp.zeros((2, 1, LANES), F32).at[0, 0, :SSD_HEADS].set(a[0]).at[
        1, 0, :SSD_HEADS].set(b[0])
    head_major = lambda a: jnp.broadcast_to(jnp.swapaxes(a, 1, 2), (2, LANES, LANES))
    alog = pad_heads(a_log_fwd, a_log_bwd)
    dtb = pad_heads(dt_bias_fwd, dt_bias_bwd)
    dsk = jnp.zeros((2, 1, SSD_W), F32).at[0, 0].set(jnp.repeat(d_skip[0], SSD_HEAD_DIM))
    lower = np.tril(np.ones((CHUNK, CHUNK), np.float32))
    return dict(
        g_mix=g_mix_norm[0][None], wu=w[:, :s1].astype(BF16), wz=w[:, s1:s2].astype(BF16),
        wx=w[:, s2:s3].astype(BF16), wdt=wdt.astype(BF16), wdtt=wdt.T.astype(BF16),
        cs=_channel_dft_table(), conv_w=conv_w[0, :, 0, :], conv_b=conv_b[0][None],
        tri=jnp.asarray(np.stack([lower, lower.T])), alog=alog, alogt=head_major(alog),
        dtb=dtb, dtbt=head_major(dtb), dsk=dsk,
        g_four=g_fourier_norm[0][None], g_ssd=g_ssd_norm[0][None],
        w4=w_out[0, :FOURIER_W].astype(BF16), wy=w_out[0, FOURIER_W:].astype(BF16),
        g_x=g_xattn_norm[0][None], g_mem=g_mem_norm[0][None],
        wq=w_q[0].astype(BF16), wk=w_k[0].astype(BF16), wv=w_v[0].astype(BF16), wo=w_o[0].astype(BF16),
        g_moe=g_moe_norm[0][None], wr_t=w_router[0].T, br=b_router[0][:, None],
        w1=w_mlp1[0].astype(BF16), b1=b_mlp1[0][:, None, :], w2=w_mlp2[0].astype(BF16),
        b2=b_mlp2[0][:, None, :], g_final=g_final[None])


def _encode_group(x, mem, p):
    bsz, seqlen, _ = x.shape
    t = bsz * seqlen
    tm = min(TOKEN_TILE, seqlen // 2)
    assert seqlen % (2 * tm) == 0 and tm % MOE_TILE == 0 and seqlen % (SSD_SUB * CHUNK) == 0, x.shape

    ab, z, xbc, dt, dtt = _inproj(x, p["g_mix"], p["wu"], p["wz"], p["wx"], p["wdt"], p["wdtt"],
                                  p["cs"], p["conv_w"], p["conv_b"], tm)
    cos_t, nsin_t, cmid_t = _dft_tables(seqlen)
    nb = 4 if bsz % 4 == 0 else (2 if bsz % 2 == 0 else 1)
    four = _fourier(ab.reshape(bsz, seqlen, 2 * FOURIER_W), cos_t, nsin_t, cmid_t, p["g_four"], nb,
                    tm, min(DFT_TILE, seqlen))
    yf, yb = _ssd(xbc.reshape(bsz, seqlen, CONV_CH), dt.reshape(bsz, seqlen, 2 * LANES), dtt,
                  p["tri"], p["alog"], p["alogt"], p["dtb"], p["dtbt"], p["dsk"])

    k, v = _kv(mem, p["g_mem"], p["wk"], p["wv"])
    h, xn, rloc, gate, cnt, car = _post(x, four, yf, yb, z.reshape(bsz, seqlen, SSD_W), k, v, p, tm)
    h = h.reshape(t, D_MODEL)
    xn = xn.reshape(t, D_MODEL)

    td = MOE_TILE
    counts = jnp.sum(cnt, axis=1)
    padded = ((counts + MOE_BLK - 1) // MOE_BLK) * MOE_BLK
    pend = jnp.cumsum(padded)
    pstart = (pend - padded).astype(jnp.int32)
    max_rows = t * TOP_K + (t // td) * N_EXPERTS * (RUN_ALIGN - 1)
    n_blocks = -(-max_rows // MOE_BLK) + N_EXPERTS
    n_slots = n_blocks * MOE_BLK
    n_active = (pend[-1] // MOE_BLK).astype(jnp.int32)
    blk = jnp.arange(n_blocks, dtype=jnp.int32)
    be = jnp.sum(pend[None, :] <= (jnp.minimum(blk, n_active - 1) * MOE_BLK)[:, None], axis=1)
    be = jnp.minimum(be, N_EXPERTS - 1).astype(jnp.int32)
    zstart = (pend - MOE_BLK).astype(jnp.int32)[jnp.argsort(padded == 0)]
    nzero = jnp.stack([jnp.sum(padded > 0).astype(jnp.int32), n_active])

    cnt_flat = cnt.T.reshape(-1)
    car_flat = car.T.reshape(-1)
    xs = _dispatch(pstart, cnt_flat, car_flat, zstart, nzero, xn, rloc, n_slots, td)
    outs = _experts(be, n_active[None], xs, p["w1"], p["b1"], p["w2"], p["b2"])
    y = _combine(pstart, cnt_flat, car_flat, h, rloc, gate, p["g_final"], outs, td)
    return y.reshape(bsz, seqlen, D_MODEL)


def kernel(x_prompt, x_sample, mem_prompt, mem_sample, g_mix_norm, w_in, conv_w, conv_b, a_log_fwd, a_log_bwd, dt_bias_fwd, dt_bias_bwd, d_skip, g_fourier_norm, g_ssd_norm, w_out, g_xattn_norm, g_mem_norm, w_q, w_k, w_v, w_o, g_moe_norm, w_router, b_router, w_mlp1, b_mlp1, w_mlp2, b_mlp2, g_final):
    p = _prep_weights(g_mix_norm, w_in, conv_w, conv_b, a_log_fwd, a_log_bwd, dt_bias_fwd,
                      dt_bias_bwd, d_skip, g_fourier_norm, g_ssd_norm, w_out, g_xattn_norm,
                      g_mem_norm, w_q, w_k, w_v, w_o, g_moe_norm, w_router, b_router, w_mlp1,
                      b_mlp1, w_mlp2, b_mlp2, g_final)
    return (_encode_group(x_prompt, mem_prompt, p), _encode_group(x_sample, mem_sample, p))
```
